```python
import jax, jax.numpy as jnp
from jax import lax
import numpy as np

D_MODEL = 1024
BATCH = 1
SEQ = 16384
DEPTH = 1
DEC_BATCH = 128
DEC_SEQ = 8
PAST_LEN = 8192
PAGE_SIZE = 128

N_HEADS = 8
HEAD_DIM = 64
ATTN_W = N_HEADS * HEAD_DIM
POOL_W = D_MODEL - ATTN_W
POOL_WINDOWS = (2, 4, 8, 16)
N_POOL_GROUPS = len(POOL_WINDOWS)
POOL_GROUP_W = POOL_W // N_POOL_GROUPS
POOL_HIST = max(POOL_WINDOWS) - 1
IN_W = 3 * ATTN_W + POOL_W
MOBA_BLOCK = 256
MOBA_TOPK = 3
Q_BLOCK = 128
ROPE_THETA = 10000.0
D_FF = 2816
CONV_W = 3
P_DIM = 256
EPS = 1e-6

kernel_name = 'hymba_moba_pool_convffn_step'

F32 = jnp.float32


def rmsnorm(x, g):
    xf = x.astype(F32)
    y = xf * lax.rsqrt(jnp.mean(xf * xf, axis=-1, keepdims=True) + EPS)
    return (y * g.astype(F32)).astype(x.dtype)


def rope(x, pos):
    inv = ROPE_THETA ** (-jnp.arange(0, HEAD_DIM, 2, dtype=F32) / HEAD_DIM)
    ang = pos.astype(F32)[:, None] * inv[None, :]
    cos = jnp.cos(ang)[:, None, :]
    sin = jnp.sin(ang)[:, None, :]
    xf = x.astype(F32)
    x1, x2 = xf[..., :HEAD_DIM // 2], xf[..., HEAD_DIM // 2:]
    return jnp.concatenate([x1 * cos - x2 * sin, x2 * cos + x1 * sin], axis=-1).astype(x.dtype)


def select_blocks(q, k_bar, n_past_blocks, topk):
    s = jnp.einsum('bqhd,bjhd->bhqj', q.astype(F32), k_bar.astype(F32))
    j = jnp.arange(k_bar.shape[1])
    s = jnp.where(j < n_past_blocks, s, -jnp.inf)
    _, idx = lax.top_k(s, topk)
    return idx, idx < n_past_blocks


def moba_attend(q, k_sel, v_sel, valid, k_own, v_own, own_mask):
    qf = q.astype(F32) * (HEAD_DIM ** -0.5)
    s_own = jnp.einsum('bqhd,bkhd->bhqk', qf, k_own.astype(F32))
    s_own = jnp.where(own_mask, s_own, -jnp.inf)
    if k_sel is None:
        p = jax.nn.softmax(s_own, axis=-1)
        o = jnp.einsum('bhqk,bkhd->bqhd', p, v_own.astype(F32))
        return o.astype(q.dtype)
    s_sel = jnp.einsum('bqhd,bhqtkd->bhqtk', qf, k_sel.astype(F32))
    s_sel = jnp.where(valid[..., None], s_sel, -jnp.inf)
    b_sz, h, nq, t, mb = s_sel.shape
    s = jnp.concatenate([s_sel.reshape(b_sz, h, nq, t * mb), s_own], axis=-1)
    p = jax.nn.softmax(s, axis=-1)
    p_sel = p[..., :t * mb].reshape(b_sz, h, nq, t, mb)
    p_own = p[..., t * mb:]
    o = (jnp.einsum('bhqtk,bhqtkd->bqhd', p_sel, v_sel.astype(F32))
         + jnp.einsum('bhqk,bkhd->bqhd', p_own, v_own.astype(F32)))
    return o.astype(q.dtype)


def moba_prompt(q, k, v):
    b_sz, s_len = q.shape[0], q.shape[1]
    n_full = s_len // MOBA_BLOCK
    topk = min(MOBA_TOPK, n_full)
    lead = MOBA_BLOCK - Q_BLOCK
    pad = jnp.zeros((b_sz, lead, N_HEADS, HEAD_DIM), k.dtype)
    k_pad = jnp.concatenate([pad, k], axis=1)
    v_pad = jnp.concatenate([pad, v], axis=1)
    if topk > 0:
        k_blk = k[:, :n_full * MOBA_BLOCK].reshape(b_sz, n_full, MOBA_BLOCK, N_HEADS, HEAD_DIM)
        v_blk = v[:, :n_full * MOBA_BLOCK].reshape(b_sz, n_full, MOBA_BLOCK, N_HEADS, HEAD_DIM)
        k_bar = jnp.mean(k_blk.astype(F32), axis=2)
    bi = jnp.arange(b_sz)[:, None, None, None, None]
    hi = jnp.arange(N_HEADS)[None, :, None, None, None]
    ri = jnp.arange(MOBA_BLOCK)

    def one_block(n):
        s0 = n * Q_BLOCK
        qb = lax.dynamic_slice_in_dim(q, s0, Q_BLOCK, axis=1)
        kw = lax.dynamic_slice_in_dim(k_pad, s0, MOBA_BLOCK, axis=1)
        vw = lax.dynamic_slice_in_dim(v_pad, s0, MOBA_BLOCK, axis=1)
        b_own = s0 // MOBA_BLOCK
        pos_q = s0 + jnp.arange(Q_BLOCK)
        pos_k = s0 - lead + jnp.arange(MOBA_BLOCK)
        own_mask = (pos_k[None, :] >= b_own * MOBA_BLOCK) & (pos_k[None, :] <= pos_q[:, None])
        if topk == 0:
            return moba_attend(qb, None, None, None, kw, vw, own_mask)
        idx, valid = select_blocks(qb, k_bar, b_own, topk)
        sel = (bi, idx[..., None], ri, hi)
        return moba_attend(qb, k_blk[sel], v_blk[sel], valid, kw, vw, own_mask)

    out = lax.map(one_block, jnp.arange(s_len // Q_BLOCK))
    return jnp.moveaxis(out, 0, 1).reshape(b_sz, s_len, ATTN_W)


def moba_sample(q, k, v, cache_k, cache_v, page_table, layer):
    n_new = q.shape[1]
    b_own = PAST_LEN // MOBA_BLOCK
    own_start = b_own * MOBA_BLOCK
    topk = min(MOBA_TOPK, b_own)
    pages_per_block = MOBA_BLOCK // PAGE_SIZE
    n_own_cached = PAST_LEN - own_start
    pos_q = PAST_LEN + jnp.arange(n_new)
    pos_k = own_start + jnp.arange(n_own_cached + n_new)
    own_mask = pos_k[None, :] <= pos_q[:, None]
    hi4 = jnp.arange(N_HEADS)[:, None, None, None]
    hi5 = jnp.arange(N_HEADS)[:, None, None, None, None]
    ri = jnp.arange(MOBA_BLOCK)
    ci = jnp.arange(PAGE_SIZE)
    pi = jnp.arange(pages_per_block)

    def one_seq(args):
        qs, ks, vs, pt = args
        k_past = cache_k[pt, layer].reshape(PAST_LEN, N_HEADS, HEAD_DIM)
        v_own_c = cache_v[pt[own_start // PAGE_SIZE:], layer].reshape(n_own_cached, N_HEADS, HEAD_DIM)
        k_own = jnp.concatenate([k_past[own_start:], ks], axis=0)[None]
        v_own = jnp.concatenate([v_own_c, vs], axis=0)[None]
        if topk == 0:
            return moba_attend(qs[None], None, None, None, k_own, v_own, own_mask)[0]
        k_blk = k_past[:own_start].reshape(b_own, MOBA_BLOCK, N_HEADS, HEAD_DIM)
        k_bar = jnp.mean(k_blk.astype(F32), axis=1)
        idx, valid = select_blocks(qs[None], k_bar[None], b_own, topk)
        ib = idx[0]
        k_sel = k_blk[ib[..., None], ri, hi4]
        phys = pt[ib[..., None] * pages_per_block + pi]
        v_sel = cache_v[phys[..., None], layer, ci, hi5].reshape(N_HEADS, n_new, topk, MOBA_BLOCK, HEAD_DIM)
        return moba_attend(qs[None], k_sel[None], v_sel[None], valid, k_own, v_own, own_mask)[0]

    out = lax.map(one_seq, (q, k, v, page_table))
    return out.reshape(q.shape[0], n_new, ATTN_W)


def pool_mix(u_ext, start_pos, w_pool, pool_scale):
    b_sz = u_ext.shape[0]
    n_new = u_ext.shape[1] - POOL_HIST
    uf = u_ext.astype(F32)
    c = jnp.concatenate([jnp.zeros_like(uf[:, :1]), jnp.cumsum(uf, axis=1)], axis=1)
    pos = start_pos + jnp.arange(n_new)
    diffs = []
    for g, w in enumerate(POOL_WINDOWS):
        sl = slice(g * POOL_GROUP_W, (g + 1) * POOL_GROUP_W)
        total = c[:, POOL_HIST + 1:, sl] - c[:, POOL_HIST + 1 - w:POOL_HIST + 1 - w + n_new, sl]
        cnt = jnp.minimum(w, pos + 1).astype(F32)[None, :, None]
        diffs.append(total / cnt - uf[:, POOL_HIST:, sl])
    d = jnp.stack(diffs, axis=2)
    y = jnp.einsum('blgc,gce->blge', d, w_pool.astype(F32)).reshape(b_sz, n_new, POOL_W)
    return (y * pool_scale.astype(F32)).astype(u_ext.dtype)


def conv_ffn(h, up_hist, w_up, conv_w, conv_b, w_down):
    up = h @ w_up
    n_new = up.shape[1]
    ext = jnp.concatenate([up_hist.astype(up.dtype), up], axis=1)
    c = conv_b
    for i in range(CONV_W):
        c = c + ext[:, i:i + n_new] * conv_w[i]
    gate, val = jnp.split(c, 2, axis=-1)
    return (jax.nn.silu(gate) * val) @ w_down, ext[:, -(CONV_W - 1):]


def mixer_inputs(x, g, w_in, pos):
    b_sz, n, _ = x.shape
    z = rmsnorm(x, g) @ w_in
    q = rope(z[..., :ATTN_W].reshape(b_sz, n, N_HEADS, HEAD_DIM), pos)
    k = rope(z[..., ATTN_W:2 * ATTN_W].reshape(b_sz, n, N_HEADS, HEAD_DIM), pos)
    v = z[..., 2 * ATTN_W:3 * ATTN_W].reshape(b_sz, n, N_HEADS, HEAD_DIM)
    u = z[..., 3 * ATTN_W:]
    return q, k, v, u


def post_mixer(x, a, pooled, w_out, g_ffn, up_hist, w_up, conv_w, conv_b, w_down, p, w_pg, w_pp):
    x = x + jnp.concatenate([a, pooled], axis=-1) @ w_out
    f, conv_state = conv_ffn(rmsnorm(x, g_ffn), up_hist, w_up, conv_w, conv_b, w_down)
    x = x + f
    x = x + jax.nn.sigmoid(x @ w_pg) * (p @ w_pp)
    return x, conv_state


def setup_inputs(seed: int = 0) -> dict:
    key = jax.random.key(seed)
    ks = jax.random.split(key, 24)
    n_pages = PAST_LEN // PAGE_SIZE
    n_used = DEC_BATCH * n_pages
    n_phys = n_used + n_used // 4

    def nrm(k, shape, scale):
        return jax.random.normal(k, shape, F32) * scale

    return {
        'x_prompt': nrm(ks[0], (BATCH, SEQ, D_MODEL), 1.0),
        'x_sample': nrm(ks[1], (DEC_BATCH, DEC_SEQ, D_MODEL), 1.0),
        'cache_k': nrm(ks[2], (n_phys, DEPTH, PAGE_SIZE, N_HEADS, HEAD_DIM), 1.0),
        'cache_v': nrm(ks[3], (n_phys, DEPTH, PAGE_SIZE, N_HEADS, HEAD_DIM), 1.0),
        'state_pool': nrm(ks[4], (DEC_BATCH, DEPTH, POOL_HIST, POOL_W), 1.0),
        'state_conv': nrm(ks[5], (DEC_BATCH, DEPTH, CONV_W - 1, 2 * D_FF), 1.0),
        'page_table': jax.random.permutation(ks[6], n_phys)[:n_used].reshape(DEC_BATCH, n_pages).astype(jnp.int32),
        'p_prompt': nrm(ks[7], (DEPTH, BATCH, SEQ, P_DIM), 1.0),
        'p_sample': nrm(ks[8], (DEPTH, DEC_BATCH, DEC_SEQ, P_DIM), 1.0),
        'g_mix': 1.0 + nrm(ks[9], (DEPTH, D_MODEL), 0.02),
        'w_in': nrm(ks[10], (DEPTH, D_MODEL, IN_W), D_MODEL ** -0.5),
        'w_out': nrm(ks[11], (DEPTH, D_MODEL, D_MODEL), D_MODEL ** -0.5),
        'w_pool': nrm(ks[12], (DEPTH, N_POOL_GROUPS, POOL_GROUP_W, POOL_GROUP_W), POOL_GROUP_W ** -0.5),
        'pool_scale': 1.0 + nrm(ks[13], (DEPTH, POOL_W), 0.02),
        'g_ffn': 1.0 + nrm(ks[14], (DEPTH, D_MODEL), 0.02),
        'w_up': nrm(ks[15], (DEPTH, D_MODEL, 2 * D_FF), D_MODEL ** -0.5),
        'conv_w': nrm(ks[16], (DEPTH, CONV_W, 2 * D_FF), CONV_W ** -0.5),
        'conv_b': nrm(ks[17], (DEPTH, 2 * D_FF), 0.01),
        'w_down': nrm(ks[18], (DEPTH, D_FF, D_MODEL), D_FF ** -0.5),
        'w_pg': nrm(ks[19], (DEPTH, D_MODEL, D_MODEL), D_MODEL ** -0.5),
        'w_pp': nrm(ks[20], (DEPTH, P_DIM, D_MODEL), P_DIM ** -0.5),
        'g_final': 1.0 + nrm(ks[21], (D_MODEL,), 0.02),
    }


def reference(x_prompt, x_sample, cache_k, cache_v, state_pool, state_conv, page_table,
              p_prompt, p_sample, g_mix, w_in, w_out, w_pool, pool_scale, g_ffn,
              w_up, conv_w, conv_b, w_down, w_pg, w_pp, g_final):
    b_p, s_p = x_prompt.shape[0], x_prompt.shape[1]
    n_s = x_sample.shape[1]
    pos_p = jnp.arange(s_p)
    pos_s = PAST_LEN + jnp.arange(n_s)
    xp, xs = x_prompt, x_sample
    kp_l, vp_l, ks_l, vs_l, poolp_l, pools_l, convp_l, convs_l = [], [], [], [], [], [], [], []
    for i in range(DEPTH):
        qp, kp, vp, up_ = mixer_inputs(xp, g_mix[i], w_in[i], pos_p)
        qs, kss, vss, us = mixer_inputs(xs, g_mix[i], w_in[i], pos_s)
        ap = moba_prompt(qp, kp, vp)
        asmp = moba_sample(qs, kss, vss, cache_k, cache_v, page_table, i)
        up_ext = jnp.concatenate([jnp.zeros((b_p, POOL_HIST, POOL_W), up_.dtype), up_], axis=1)
        us_ext = jnp.concatenate([state_pool[:, i].astype(us.dtype), us], axis=1)
        pp = pool_mix(up_ext, 0, w_pool[i], pool_scale[i])
        ps = pool_mix(us_ext, PAST_LEN, w_pool[i], pool_scale[i])
        zero_hist = jnp.zeros((b_p, CONV_W - 1, 2 * D_FF), xp.dtype)
        xp, cp = post_mixer(xp, ap, pp, w_out[i], g_ffn[i], zero_hist, w_up[i], conv_w[i], conv_b[i],
                            w_down[i], p_prompt[i], w_pg[i], w_pp[i])
        xs, cs = post_mixer(xs, asmp, ps, w_out[i], g_ffn[i], state_conv[:, i], w_up[i], conv_w[i], conv_b[i],
                            w_down[i], p_sample[i], w_pg[i], w_pp[i])
        kp_l.append(kp)
        vp_l.append(vp)
        ks_l.append(kss)
        vs_l.append(vss)
        poolp_l.append(up_ext[:, -POOL_HIST:])
        pools_l.append(us_ext[:, -POOL_HIST:])
        convp_l.append(cp)
        convs_l.append(cs)
    y_prompt = rmsnorm(xp, g_final)
    y_sample = rmsnorm(xs, g_final)
    return (y_prompt, y_sample,
            jnp.stack(kp_l, axis=1), jnp.stack(vp_l, axis=1),
            jnp.stack(ks_l, axis=1), jnp.stack(vs_l, axis=1),
            jnp.stack(poolp_l, axis=1), jnp.stack(pools_l, axis=1),
            jnp.stack(convp_l, axis=1), jnp.stack(convs_l, axis=1))
```

```python
import functools

import jax
import jax.numpy as jnp
from jax import lax
from jax.experimental import pallas as pl
from jax.experimental.pallas import tpu as pltpu

F32 = jnp.float32
BF16 = jnp.bfloat16

MOBA_BLOCK = 256
MOBA_TOPK = 3
ROPE_THETA = 10000.0
POOL_WINDOWS = (2, 4, 8, 16)
CONV_W = 3
EPS = 1e-6
MASKED = -1e30

LANES = 128
BF16_SUBLANES = 16
VMEM_LIMIT = 56 * 1024 * 1024

IN_PROJ_ROWS = 512
MIX_ROWS = 512
FFN_ROWS = 1024
FFN_CHUNK = 256
SAMPLE_PAGES_PER_STEP = 16

_NT = (((1,), (1,)), ((), ()))


def _params(*sem):
    return pltpu.CompilerParams(dimension_semantics=sem, vmem_limit_bytes=VMEM_LIMIT)


def _rmsnorm(x, g):
    return x * lax.rsqrt(jnp.mean(x * x, axis=-1, keepdims=True) + EPS) * g


def _top_k_mask(s, idx, k, axis, n):
    picked = jnp.zeros(s.shape, jnp.bool_)
    for _ in range(k):
        mx = jnp.max(s, axis=axis, keepdims=True)
        first = jnp.min(jnp.where(s == mx, idx, float(n)), axis=axis, keepdims=True)
        hit = idx == first
        picked = picked | hit
        s = jnp.where(hit, -jnp.inf, s)
    return picked


def _in_proj_kernel(x_ref, g_ref, w_ref, cos_ref, sin_ref, *outs, attn_w, head_dim, prompt):
    tm = x_ref.shape[0]
    xn = _rmsnorm(x_ref[...], g_ref[...])
    z = jnp.dot(xn.astype(BF16), w_ref[...], preferred_element_type=F32)
    reps = attn_w // LANES
    cos = jnp.tile(cos_ref[...], (1, reps))
    sin = jnp.tile(sin_ref[...], (1, reps))
    half = head_dim // 2
    lane = lax.broadcasted_iota(jnp.int32, (tm, attn_w), 1)
    lower = (lane % head_dim) < half

    def rope(t):
        rot = jnp.where(lower, pltpu.roll(t, attn_w - half, 1), pltpu.roll(t, half, 1))
        return t * cos + rot * sin

    q = rope(z[:, :attn_w]) * (head_dim ** -0.5)
    k = rope(z[:, attn_w:2 * attn_w])
    v = z[:, 2 * attn_w:3 * attn_w]
    u = z[:, 3 * attn_w:]
    if prompt:
        qt_ref, k_ref, v_ref, u_ref, kb_ref, vt_ref, kbar_ref = outs
        qt_ref[...] = q.T
        kb_ref[...] = k.astype(BF16)
        vt = v.T.astype(BF16)
        for i in range(tm // MOBA_BLOCK):
            rows = slice(i * MOBA_BLOCK, (i + 1) * MOBA_BLOCK)
            vt_ref[i] = vt[:, rows]
            kbar_ref[i] = jnp.sum(k[rows], axis=0, keepdims=True) * (1.0 / MOBA_BLOCK)
    else:
        q_ref, k_ref, v_ref, u_ref = outs
        q_ref[...] = q
    k_ref[...] = k
    v_ref[...] = v
    u_ref[...] = u


def _rope_tables(pos, head_dim):
    inv = ROPE_THETA ** (-jnp.arange(0, head_dim, 2, dtype=F32) / head_dim)
    ang = pos.astype(F32)[:, None] * inv[None, :]
    c, s = jnp.cos(ang), jnp.sin(ang)
    reps = LANES // head_dim
    return jnp.tile(jnp.concatenate([c, c], 1), (1, reps)), jnp.tile(jnp.concatenate([-s, s], 1), (1, reps))


def _in_proj(x, g, w_bf, pos, attn_w, head_dim, prompt):
    rows, d = x.shape
    tm = min(IN_PROJ_ROWS, rows)
    cos, sin = _rope_tables(pos, head_dim)
    row_blk = lambda w: pl.BlockSpec((tm, w), lambda i: (i, 0))
    full = lambda a: pl.BlockSpec(a.shape, lambda i: (0,) * a.ndim)
    nat = jax.ShapeDtypeStruct((rows, attn_w), F32)
    if prompt:
        nblk, bpt = rows // MOBA_BLOCK, tm // MOBA_BLOCK
        out_shape = [jax.ShapeDtypeStruct((attn_w, rows), F32), nat, nat, nat,
                     jax.ShapeDtypeStruct((rows, attn_w), BF16),
                     jax.ShapeDtypeStruct((nblk, attn_w, MOBA_BLOCK), BF16),
                     jax.ShapeDtypeStruct((nblk, 1, attn_w), F32)]
        out_specs = [pl.BlockSpec((attn_w, tm), lambda i: (0, i)), row_blk(attn_w), row_blk(attn_w),
                     row_blk(attn_w), row_blk(attn_w),
                     pl.BlockSpec((bpt, attn_w, MOBA_BLOCK), lambda i: (i, 0, 0)),
                     pl.BlockSpec((bpt, 1, attn_w), lambda i: (i, 0, 0))]
    else:
        out_shape = [nat, nat, nat, nat]
        out_specs = [row_blk(attn_w)] * 4
    return pl.pallas_call(
        functools.partial(_in_proj_kernel, attn_w=attn_w, head_dim=head_dim, prompt=prompt),
        grid=(rows // tm,),
        in_specs=[row_blk(d), full(g), full(w_bf), row_blk(LANES), row_blk(LANES)],
        out_specs=out_specs, out_shape=out_shape,
        compiler_params=_params("parallel"),
        name="in_proj_prompt" if prompt else "in_proj_sample",
    )(x, g, w_bf, cos, sin)


def _moba_prompt_kernel(qt_ref, kb_ref, vt_ref, kbar_ref, o_ref,
                        qm_ref, bias_ref, m_ref, l_ref, acc_ref, *, head_dim, heads):
    t = pl.program_id(1)
    nblk = kbar_ref.shape[0]
    tq = qt_ref.shape[1]
    qt = qt_ref[...]
    row_head = lax.broadcasted_iota(jnp.int32, qt.shape, 0) // head_dim
    blk_idx = lax.broadcasted_iota(jnp.int32, (nblk, tq), 0)
    past = blk_idx < t
    key_pos = lax.broadcasted_iota(jnp.int32, (MOBA_BLOCK, tq), 0)
    qry_pos = lax.broadcasted_iota(jnp.int32, (MOBA_BLOCK, tq), 1)
    causal = key_pos <= qry_pos
    kbar = kbar_ref[...]
    k_own = kb_ref[t]
    topk = min(MOBA_TOPK, nblk)

    def v_t(j, h):
        return vt_ref[j, h * head_dim:(h + 1) * head_dim, :]

    for h in range(heads):
        qm = jnp.where(row_head == h, qt, 0.0)
        qm_ref[h] = qm.astype(BF16)
        sc = jnp.dot(kbar, qm, precision=lax.Precision.HIGHEST, preferred_element_type=F32)
        picked = _top_k_mask(jnp.where(past, sc, -jnp.inf), blk_idx.astype(F32), topk, 0, nblk)
        bias_ref[h] = jnp.where(picked & past, 0.0, MASKED)
        s = jnp.dot(k_own, qm_ref[h], preferred_element_type=F32)
        s = jnp.where(causal, s, MASKED)
        m0 = jnp.max(s, axis=0, keepdims=True)
        p = jnp.exp(s - m0)
        m_ref[h] = m0
        l_ref[h] = jnp.sum(p, axis=0, keepdims=True)
        acc_ref[h] = jnp.dot(v_t(t, h), p.astype(BF16), preferred_element_type=F32)

    def body(j, carry):
        kj = kb_ref[j]
        for h in range(heads):
            s = jnp.dot(kj, qm_ref[h], preferred_element_type=F32) + bias_ref[h, pl.ds(j, 1), :]
            m_prev = m_ref[h]
            m_new = jnp.maximum(m_prev, jnp.max(s, axis=0, keepdims=True))
            alpha = jnp.exp(m_prev - m_new)
            p = jnp.exp(s - m_new)
            l_ref[h] = alpha * l_ref[h] + jnp.sum(p, axis=0, keepdims=True)
            acc_ref[h] = alpha * acc_ref[h] + jnp.dot(v_t(j, h), p.astype(BF16),
                                                      preferred_element_type=F32)
            m_ref[h] = m_new
        return carry

    lax.fori_loop(0, t, body, 0)
    o = jnp.concatenate([acc_ref[h] / l_ref[h] for h in range(heads)], axis=0)
    o_ref[...] = o.T.astype(o_ref.dtype)


def _moba_prompt(qt, kb, vt, kbar, head_dim):
    attn_w, s_len = qt.shape
    nblk = s_len // MOBA_BLOCK
    gw = 2 * LANES
    heads = gw // head_dim
    kb3 = kb.reshape(nblk, MOBA_BLOCK, attn_w)
    return pl.pallas_call(
        functools.partial(_moba_prompt_kernel, head_dim=head_dim, heads=heads),
        grid=(attn_w // gw, nblk),
        in_specs=[pl.BlockSpec((gw, MOBA_BLOCK), lambda g, t: (g, t)),
                  pl.BlockSpec((nblk, MOBA_BLOCK, gw), lambda g, t: (0, 0, g)),
                  pl.BlockSpec((nblk, gw, MOBA_BLOCK), lambda g, t: (0, g, 0)),
                  pl.BlockSpec((nblk, gw), lambda g, t: (0, g))],
        out_specs=pl.BlockSpec((MOBA_BLOCK, gw), lambda g, t: (t, g)),
        out_shape=jax.ShapeDtypeStruct((s_len, attn_w), BF16),
        scratch_shapes=[pltpu.VMEM((heads, gw, MOBA_BLOCK), BF16),
                        pltpu.VMEM((heads, nblk, MOBA_BLOCK), F32),
                        pltpu.VMEM((heads, 1, MOBA_BLOCK), F32),
                        pltpu.VMEM((heads, 1, MOBA_BLOCK), F32),
                        pltpu.VMEM((heads, head_dim, MOBA_BLOCK), F32)],
        compiler_params=_params("parallel", "arbitrary"),
        name="moba_prompt",
    )(qt, kb3, vt, kbar)


def _moba_sample_kernel(pt_ref, q_ref, kn_ref, vn_ref, *rest, pps, n_heads, head_dim):
    del pt_ref
    k_refs, v_refs = rest[:pps], rest[pps:2 * pps]
    o_ref, s_ref, p_ref, ksum_ref, acc_ref, qbd_ref, qbf_ref, l_ref, pown_ref = rest[2 * pps:]
    b, c = pl.program_id(0), pl.program_id(1)
    n_seq, n_chunks = pl.num_programs(0) - 1, pl.num_programs(1)
    lq, width = q_ref.shape
    page = k_refs[0].shape[0]
    ncol = n_heads * lq
    ppb = MOBA_BLOCK // page
    bps = pps // ppb
    nblk = ksum_ref.shape[0]

    @pl.when(b > 0)
    def _apply_values():
        acc = jnp.zeros((ncol, width), F32)
        for i in range(pps):
            acc += jnp.dot(p_ref[c, :, i * page:(i + 1) * page], v_refs[i][...].astype(BF16),
                           preferred_element_type=F32)

        @pl.when(c == 0)
        def _():
            acc_ref[...] = acc

        @pl.when(c > 0)
        def _():
            acc_ref[...] += acc

        @pl.when(c == n_chunks - 1)
        def _finish():
            o = acc_ref[...]
            vn, pown = vn_ref[...], pown_ref[...]
            for j in range(lq):
                o = o + pown[:, j:j + 1] * vn[j:j + 1, :]
            o = o / l_ref[...]
            lane_head = lax.broadcasted_iota(jnp.int32, (lq, width), 1) // head_dim
            out = jnp.zeros((lq, width), F32)
            for h in range(n_heads):
                out = jnp.where(lane_head == h, o[h * lq:(h + 1) * lq, :], out)
            o_ref[...] = out.astype(o_ref.dtype)

    @pl.when(b < n_seq)
    def _score_keys():
        @pl.when(c == 0)
        def _():
            q_rep = jnp.concatenate([q_ref[...]] * n_heads, axis=0)
            row_head = lax.broadcasted_iota(jnp.int32, (ncol, width), 0) // lq
            lane_head = lax.broadcasted_iota(jnp.int32, (ncol, width), 1) // head_dim
            qbd = jnp.where(row_head == lane_head, q_rep, 0.0)
            qbf_ref[...] = qbd
            qbd_ref[...] = qbd.astype(BF16)

        qbd = qbd_ref[...]
        for i in range(pps):
            kp = k_refs[i][...]
            part = jnp.sum(kp.reshape(page // 8, 8, width), axis=0)
            blk_sum = part if i % ppb == 0 else blk_sum + part
            if i % ppb == ppb - 1:
                ksum_ref[c * bps + i // ppb] = blk_sum
            s_ref[c, :, i * page:(i + 1) * page] = lax.dot_general(
                qbd, kp.astype(BF16), _NT, preferred_element_type=F32)

        @pl.when(c == n_chunks - 1)
        def _select_and_normalise():
            kbar = jnp.sum(ksum_ref[...], axis=1) * (1.0 / MOBA_BLOCK)
            sc = lax.dot_general(qbf_ref[...], kbar, _NT, precision=lax.Precision.HIGHEST,
                                 preferred_element_type=F32)
            blk_idx = lax.broadcasted_iota(jnp.int32, (ncol, nblk), 1)
            picked = _top_k_mask(sc, blk_idx.astype(F32), min(MOBA_TOPK, nblk), 1, nblk)
            bias = jnp.where(picked, 0.0, MASKED)
            s_own = lax.dot_general(qbd, kn_ref[...].astype(BF16), _NT, preferred_element_type=F32)
            qry = lax.broadcasted_iota(jnp.int32, (ncol, lq), 0) % lq
            key = lax.broadcasted_iota(jnp.int32, (ncol, lq), 1)
            s_own = jnp.where(key <= qry, s_own, MASKED)
            m = jnp.max(s_own, axis=1, keepdims=True)

            def blk_scores(j):
                cj, off = j // bps, (j % bps) * MOBA_BLOCK
                return s_ref[cj, :, off:off + MOBA_BLOCK] + bias[:, j:j + 1]

            for j in range(nblk):
                m = jnp.maximum(m, jnp.max(blk_scores(j), axis=1, keepdims=True))
            p_own = jnp.exp(s_own - m)
            pown_ref[...] = p_own
            l = jnp.sum(p_own, axis=1, keepdims=True)
            for j in range(nblk):
                cj, off = j // bps, (j % bps) * MOBA_BLOCK
                p = jnp.exp(blk_scores(j) - m)
                l = l + jnp.sum(p, axis=1, keepdims=True)
                p_ref[cj, :, off:off + MOBA_BLOCK] = p.astype(BF16)
            l_ref[...] = l


def _moba_sample(q, k_new, v_new, cache_k, cache_v, page_table, n_heads, head_dim):
    n_seq, lq, width = q.shape
    n_phys, page = cache_k.shape[0], cache_k.shape[1]
    n_pages = page_table.shape[1]
    pps = min(SAMPLE_PAGES_PER_STEP, n_pages)
    n_chunks = n_pages // pps
    nblk = n_pages * page // MOBA_BLOCK
    ncol = n_heads * lq
    last = n_seq - 1

    def seq_spec(idx):
        return pl.BlockSpec((None, lq, width), lambda b, c, pt: (idx(b), 0, 0))

    def page_spec(idx, i):
        return pl.BlockSpec((None, page, width),
                            lambda b, c, pt: (pt[idx(b) * n_pages + c * pps + i], 0, 0))

    cur = lambda b: jnp.minimum(b, last)
    prev = lambda b: jnp.maximum(b - 1, 0)
    grid_spec = pltpu.PrefetchScalarGridSpec(
        num_scalar_prefetch=1,
        grid=(n_seq + 1, n_chunks),
        in_specs=[seq_spec(cur), seq_spec(cur), seq_spec(prev)]
        + [page_spec(cur, i) for i in range(pps)] + [page_spec(prev, i) for i in range(pps)],
        out_specs=seq_spec(prev),
        scratch_shapes=[pltpu.VMEM((n_chunks, ncol, pps * page), F32),
                        pltpu.VMEM((n_chunks, ncol, pps * page), BF16),
                        pltpu.VMEM((nblk, 8, width), F32),
                        pltpu.VMEM((ncol, width), F32),
                        pltpu.VMEM((ncol, width), BF16),
                        pltpu.VMEM((ncol, width), F32),
                        pltpu.VMEM((ncol, 1), F32),
                        pltpu.VMEM((ncol, lq), F32)])
    return pl.pallas_call(
        functools.partial(_moba_sample_kernel, pps=pps, n_heads=n_heads, head_dim=head_dim),
        grid_spec=grid_spec,
        out_shape=jax.ShapeDtypeStruct((n_seq, lq, width), BF16),
        compiler_params=_params("arbitrary", "arbitrary"),
        name="moba_sample",
    )(page_table.reshape(-1), q, k_new, v_new, *([cache_k] * pps), *([cache_v] * pps))


def _mix_out_kernel(*refs, prompt, start_pos, hist):
    if prompt:
        x_ref, a_ref, u_ref, uh_ref, wpool_ref, ps_ref, wout_ref, g_ref, x2_ref, h_ref, ext_ref = refs
        tm = u_ref.shape[0]
        t = pl.program_id(0)
        ext_ref[0:hist + 1] = jnp.where(t > 0, uh_ref[...], 0.0)
        ext_ref[hist + 1:] = u_ref[...]
        pos = t * tm + lax.broadcasted_iota(jnp.int32, (tm, 1), 0)

        def shifted(i, lanes):
            return ext_ref[pl.ds(hist + 1 - i, tm), lanes]
    else:
        (x_ref, a_ref, u_ref, uh_ref, wpool_ref, ps_ref, wout_ref, g_ref,
         x2_ref, h_ref, state_ref, ext_ref) = refs
        tb, lq = u_ref.shape[0], u_ref.shape[1]
        tm = tb * lq
        ext_ref[:, 1:hist + 1, :] = uh_ref[...]
        ext_ref[:, hist + 1:, :] = u_ref[...]
        state_ref[...] = ext_ref[:, lq + 1:, :]
        pos = start_pos + lax.broadcasted_iota(jnp.int32, (tm, 1), 0) % lq

        def shifted(i, lanes):
            return ext_ref[:, pl.ds(hist + 1 - i, lq), lanes].reshape(tm, lanes.stop - lanes.start)

    gw = wpool_ref.shape[1]
    ys = []
    for g, w in enumerate(POOL_WINDOWS):
        lanes = slice(g * gw, (g + 1) * gw)
        cur = shifted(0, lanes)
        total = cur
        for i in range(1, w):
            total = total + shifted(i, lanes)
        cnt = jnp.minimum(w, pos + 1).astype(F32)
        d = total / cnt - cur
        ys.append(jnp.dot(d.astype(BF16), wpool_ref[g], preferred_element_type=F32))
    pooled = jnp.concatenate(ys, axis=1) * ps_ref[...]
    mixed = jnp.concatenate([a_ref[...], pooled.astype(BF16)], axis=1)
    x2 = x_ref[...] + jnp.dot(mixed, wout_ref[...], preferred_element_type=F32)
    x2_ref[...] = x2
    h_ref[...] = _rmsnorm(x2, g_ref[...]).astype(BF16)


def _mix_out(x, a, u, u_hist, wpool_bf, pool_scale, wout_bf, g_ffn, prompt, start_pos):
    rows, d = x.shape
    pool_w = a.shape[1]
    hist = max(POOL_WINDOWS) - 1
    full = lambda arr: pl.BlockSpec(arr.shape, lambda i: (0,) * arr.ndim)
    row_blk = lambda w: pl.BlockSpec((tm, w), lambda i: (i, 0))
    out_shape = [jax.ShapeDtypeStruct((rows, d), F32), jax.ShapeDtypeStruct((rows, d), BF16)]
    if prompt:
        tm = min(MIX_ROWS, rows)
        halo = hist + 1
        u_specs = [row_blk(pool_w),
                   pl.BlockSpec((halo, pool_w), lambda i: (jnp.maximum(i * (tm // halo) - 1, 0), 0))]
        u_args = (u, u)
        out_specs = [row_blk(d), row_blk(d)]
        scratch = [pltpu.VMEM((tm + halo, pool_w), F32)]
        grid = (rows // tm,)
    else:
        n_seq, lq = u.shape[0], u.shape[1]
        tm = rows
        u_specs = [pl.BlockSpec((n_seq, lq, pool_w), lambda i: (0, 0, 0)),
                   pl.BlockSpec((n_seq, hist, pool_w), lambda i: (0, 0, 0))]
        u_args = (u, u_hist)
        out_shape.append(jax.ShapeDtypeStruct((n_seq, hist, pool_w), F32))
        out_specs = [row_blk(d), row_blk(d), pl.BlockSpec((n_seq, hist, pool_w), lambda i: (0, 0, 0))]
        scratch = [pltpu.VMEM((n_seq, hist + 1 + lq, pool_w), F32)]
        grid = (1,)
    return pl.pallas_call(
        functools.partial(_mix_out_kernel, prompt=prompt, start_pos=start_pos, hist=hist),
        grid=grid,
        in_specs=[row_blk(d), row_blk(a.shape[1])] + u_specs
        + [full(wpool_bf), full(pool_scale), full(wout_bf), full(g_ffn)],
        out_specs=out_specs, out_shape=out_shape, scratch_shapes=scratch,
        compiler_params=_params("parallel"),
        name="mix_out_prompt" if prompt else "mix_out_sample",
    )(x, a, *u_args, wpool_bf, pool_scale, wout_bf, g_ffn)


def _ffn_tail_kernel(*refs, prompt):
    if prompt:
        (h_ref, hh_ref, x2_ref, p_ref, wg_ref, wv_ref, cwg_ref, cwv_ref, cbg_ref, cbv_ref, wd_ref,
         wpg_ref, wpp_ref, gf_ref, y_ref, sg_ref, sv_ref, hext_ref, eg_ref, ev_ref, acc_ref) = refs
    else:
        (h_ref, hg_ref, hv_ref, x2_ref, p_ref, wg_ref, wv_ref, cwg_ref, cwv_ref, cbg_ref, cbv_ref,
         wd_ref, wpg_ref, wpp_ref, gf_ref, y_ref, sg_ref, sv_ref, eg_ref, ev_ref, acc_ref) = refs
    t, c = pl.program_id(0), pl.program_id(1)
    tm = h_ref.shape[0]
    keep = CONV_W - 1
    if prompt:
        halo = hh_ref.shape[0]

        @pl.when(c == 0)
        def _():
            hext_ref[0:halo] = jnp.where(t > 0, hh_ref[...], jnp.zeros(hh_ref.shape, hh_ref.dtype))
            hext_ref[halo:] = h_ref[...]

        hx = hext_ref[...]
        eg_ref[...] = jnp.dot(hx, wg_ref[...], preferred_element_type=F32)
        ev_ref[...] = jnp.dot(hx, wv_ref[...], preferred_element_type=F32)
        sg_ref[...] = eg_ref[tm + halo - 8:, :]
        sv_ref[...] = ev_ref[tm + halo - 8:, :]

        def delayed(ref, i):
            return ref[pl.ds(halo - i, tm), :]
    else:
        tb, lq = eg_ref.shape[0], eg_ref.shape[1] - 8
        fc = eg_ref.shape[2]
        hx = h_ref[...]
        eg_ref[:, 8:, :] = jnp.dot(hx, wg_ref[...], preferred_element_type=F32).reshape(tb, lq, fc)
        ev_ref[:, 8:, :] = jnp.dot(hx, wv_ref[...], preferred_element_type=F32).reshape(tb, lq, fc)
        eg_ref[:, 8 - keep:8, :] = hg_ref[...]
        ev_ref[:, 8 - keep:8, :] = hv_ref[...]
        sg_ref[...] = eg_ref[:, 8 + lq - keep:, :]
        sv_ref[...] = ev_ref[:, 8 + lq - keep:, :]

        def delayed(ref, i):
            return ref[:, pl.ds(8 - i, lq), :].reshape(tm, fc)

    def conv(ref, cw_ref, cb_ref):
        cw = cw_ref[...]
        out = cb_ref[...]
        for i in range(CONV_W):
            out = out + delayed(ref, CONV_W - 1 - i) * cw[i:i + 1, :]
        return out

    gate = conv(eg_ref, cwg_ref, cbg_ref)
    val = conv(ev_ref, cwv_ref, cbv_ref)
    act = (gate * jax.nn.sigmoid(gate) * val).astype(BF16)
    part = jnp.dot(act, wd_ref[...], preferred_element_type=F32)

    @pl.when(c == 0)
    def _():
        acc_ref[...] = part

    @pl.when(c > 0)
    def _():
        acc_ref[...] += part

    @pl.when(c == pl.num_programs(1) - 1)
    def _():
        x3 = x2_ref[...] + acc_ref[...]
        gate_p = jax.nn.sigmoid(jnp.dot(x3.astype(BF16), wpg_ref[...], preferred_element_type=F32))
        emb = jnp.dot(p_ref[...].astype(BF16), wpp_ref[...], preferred_element_type=F32)
        y_ref[...] = _rmsnorm(x3 + gate_p * emb, gf_ref[...])


def _ffn_tail(h, x2, p, state, wup_bf, conv_w, conv_b, wdown_bf, wpg_bf, wpp_bf, g_final, prompt):
    rows, d = x2.shape
    d_ff = wdown_bf.shape[0]
    fc = FFN_CHUNK
    n_fc = d_ff // fc
    p_dim = p.shape[1]
    tm = min(FFN_ROWS, rows)
    n_t = rows // tm
    row_blk = lambda w: pl.BlockSpec((tm, w), lambda t, c: (t, 0))
    full = lambda arr: pl.BlockSpec(arr.shape, lambda t, c: (0,) * arr.ndim)
    gate_cols = lambda r: pl.BlockSpec((r, fc), lambda t, c: (0, c))
    val_cols = lambda r: pl.BlockSpec((r, fc), lambda t, c: (0, c + n_fc))
    weights = [gate_cols(d), val_cols(d), gate_cols(CONV_W), val_cols(CONV_W), gate_cols(1), val_cols(1),
               pl.BlockSpec((fc, d), lambda t, c: (c, 0)), full(wpg_bf), full(wpp_bf), full(g_final)]
    weight_args = (wup_bf, wup_bf, conv_w, conv_w, conv_b, conv_b, wdown_bf, wpg_bf, wpp_bf, g_final)
    y_shape = jax.ShapeDtypeStruct((rows, d), F32)
    if prompt:
        halo = BF16_SUBLANES
        acts = [row_blk(d),
                pl.BlockSpec((halo, d), lambda t, c: (jnp.maximum(t * (tm // halo) - 1, 0), 0)),
                row_blk(d), row_blk(p_dim)]
        act_args = (h, h, x2, p)
        st_shape = jax.ShapeDtypeStruct((n_t * 8, d_ff), F32)
        st_spec = pl.BlockSpec((8, fc), lambda t, c: (t, c))
        scratch = [pltpu.VMEM((tm + halo, d), BF16), pltpu.VMEM((tm + halo, fc), F32),
                   pltpu.VMEM((tm + halo, fc), F32), pltpu.VMEM((tm, d), F32)]
    else:
        n_seq, keep = state.shape[0], state.shape[1]
        lq = rows // n_seq
        hist_g = pl.BlockSpec((n_seq, keep, fc), lambda t, c: (0, 0, c))
        hist_v = pl.BlockSpec((n_seq, keep, fc), lambda t, c: (0, 0, c + n_fc))
        acts = [row_blk(d), hist_g, hist_v, row_blk(d), row_blk(p_dim)]
        act_args = (h, state, state, x2, p)
        st_shape = jax.ShapeDtypeStruct((n_seq, keep, d_ff), F32)
        st_spec = pl.BlockSpec((n_seq, keep, fc), lambda t, c: (0, 0, c))
        scratch = [pltpu.VMEM((n_seq, 8 + lq, fc), F32), pltpu.VMEM((n_seq, 8 + lq, fc), F32),
                   pltpu.VMEM((tm, d), F32)]
    return pl.pallas_call(
        functools.partial(_ffn_tail_kernel, prompt=prompt),
        grid=(n_t, n_fc),
        in_specs=acts + weights,
        out_specs=[row_blk(d), st_spec, st_spec],
        out_shape=[y_shape, st_shape, st_shape],
        scratch_shapes=scratch,
        compiler_params=_params("parallel", "arbitrary"),
        name="ffn_tail_prompt" if prompt else "ffn_tail_sample",
    )(*act_args, *weight_args)


def kernel(x_prompt, x_sample, cache_k, cache_v, state_pool, state_conv, page_table, p_prompt, p_sample,
           g_mix, w_in, w_out, w_pool, pool_scale, g_ffn, w_up, conv_w, conv_b, w_down, w_pg, w_pp, g_final):
    b_p, s_len, d = x_prompt.shape
    n_seq, lq, _ = x_sample.shape
    n_phys, depth, page, n_heads, head_dim = cache_k.shape
    attn_w = n_heads * head_dim
    pool_w = d - attn_w
    past_len = page_table.shape[1] * page
    hist = state_pool.shape[2]
    keep = state_conv.shape[2]
    d_ff = w_down.shape[1]
    assert b_p == 1 and depth == 1
    assert past_len % MOBA_BLOCK == 0 and MOBA_BLOCK % page == 0 and s_len % MOBA_BLOCK == 0
    assert lq == 8 and hist == max(POOL_WINDOWS) - 1 and keep == CONV_W - 1
    assert d_ff % FFN_CHUNK == 0

    i = 0
    row = lambda v: v.reshape(1, -1)
    w_in_bf, w_out_bf, w_pool_bf = w_in[i].astype(BF16), w_out[i].astype(BF16), w_pool[i].astype(BF16)
    w_up_bf, w_down_bf = w_up[i].astype(BF16), w_down[i].astype(BF16)
    w_pg_bf, w_pp_bf = w_pg[i].astype(BF16), w_pp[i].astype(BF16)
    ck = cache_k.reshape(n_phys, page, attn_w)
    cv = cache_v.reshape(n_phys, page, attn_w)

    xp = x_prompt.reshape(s_len, d)
    qt, k_p, v_p, u_p, kb, vt, kbar = _in_proj(xp, row(g_mix[i]), w_in_bf, jnp.arange(s_len),
                                               attn_w, head_dim, True)
    a_p = _moba_prompt(qt, kb, vt, kbar.reshape(-1, attn_w), head_dim)
    x2_p, h_p = _mix_out(xp, a_p, u_p, None, w_pool_bf, row(pool_scale[i]), w_out_bf, row(g_ffn[i]), True, 0)
    y_p, cg_p, cv_p = _ffn_tail(h_p, x2_p, p_prompt[i].reshape(s_len, -1), None, w_up_bf, conv_w[i],
                                row(conv_b[i]), w_down_bf, w_pg_bf, w_pp_bf, row(g_final), True)

    rows_s = n_seq * lq
    xs = x_sample.reshape(rows_s, d)
    pos_s = past_len + jnp.arange(rows_s) % lq
    q_s, k_s, v_s, u_s = _in_proj(xs, row(g_mix[i]), w_in_bf, pos_s, attn_w, head_dim, False)
    seq3 = lambda v: v.reshape(n_seq, lq, -1)
    a_s = _moba_sample(seq3(q_s), seq3(k_s), seq3(v_s), ck, cv, page_table, n_heads, head_dim)
    x2_s, h_s, pool_s = _mix_out(xs, a_s.reshape(rows_s, attn_w), seq3(u_s), state_pool[:, i], w_pool_bf,
                                 row(pool_scale[i]), w_out_bf, row(g_ffn[i]), False, past_len)
    y_s, cg_s, cv_s = _ffn_tail(h_s, x2_s, p_sample[i].reshape(rows_s, -1), state_conv[:, i], w_up_bf,
                                conv_w[i], row(conv_b[i]), w_down_bf, w_pg_bf, w_pp_bf, row(g_final), False)

    heads5 = lambda v, n, l: v.reshape(n, 1, l, n_heads, head_dim)
    conv_p = jnp.concatenate([cg_p[-keep:], cv_p[-keep:]], axis=1).reshape(1, 1, keep, 2 * d_ff)
    conv_s = jnp.concatenate([cg_s, cv_s], axis=2).reshape(n_seq, 1, keep, 2 * d_ff)
    return (y_p.reshape(1, s_len, d), y_s.reshape(n_seq, lq, d),
            heads5(k_p, 1, s_len), heads5(v_p, 1, s_len), heads5(k_s, n_seq, lq), heads5(v_s, n_seq, lq),
            u_p[-hist:].reshape(1, 1, hist, pool_w), pool_s.reshape(n_seq, 1, hist, pool_w),
            conv_p, conv_s)
```

```python
import functools
import math

import jax
import jax.numpy as jnp
from jax import lax
from jax.experimental import pallas as pl
from jax.experimental.pallas import tpu as pltpu

F32 = jnp.float32
BF16 = jnp.bfloat16

MOBA_BLOCK = 256
MOBA_TOPK = 3
ROPE_THETA = 10000.0
POOL_WINDOWS = (2, 4, 8, 16)
CONV_W = 3
EPS = 1e-6
MASKED = -1e30
LOG2_E = math.log2(math.e)

LANES = 128
BF16_SUBLANES = 16
VMEM_LIMIT = 56 * 1024 * 1024

IN_PROJ_ROWS = 512
MIX_ROWS = 512
FFN_ROWS = 1024
FFN_CHUNK = 256
SAMPLE_PAGES_PER_STEP = 32

_NT = (((1,), (1,)), ((), ()))


def _params(*sem):
    return pltpu.CompilerParams(dimension_semantics=sem, vmem_limit_bytes=VMEM_LIMIT)


def _rmsnorm(x, g):
    return x * lax.rsqrt(jnp.mean(x * x, axis=-1, keepdims=True) + EPS) * g


def _top_k_mask(s, idx, k, axis, n):
    picked = jnp.zeros(s.shape, jnp.bool_)
    for _ in range(k):
        mx = jnp.max(s, axis=axis, keepdims=True)
        first = jnp.min(jnp.where(s == mx, idx, float(n)), axis=axis, keepdims=True)
        hit = idx == first
        picked = picked | hit
        s = jnp.where(hit, -jnp.inf, s)
    return picked


def _in_proj_kernel(x_ref, g_ref, w_ref, cos_ref, sin_ref, *outs, attn_w, head_dim, prompt):
    tm = x_ref.shape[0]
    xn = _rmsnorm(x_ref[...], g_ref[...])
    z = jnp.dot(xn.astype(BF16), w_ref[...], preferred_element_type=F32)
    reps = attn_w // LANES
    cos = jnp.tile(cos_ref[...], (1, reps))
    sin = jnp.tile(sin_ref[...], (1, reps))
    half = head_dim // 2
    lane = lax.broadcasted_iota(jnp.int32, (tm, attn_w), 1)
    lower = (lane % head_dim) < half

    def rope(t):
        rot = jnp.where(lower, pltpu.roll(t, attn_w - half, 1), pltpu.roll(t, half, 1))
        return t * cos + rot * sin

    q = rope(z[:, :attn_w]) * (head_dim ** -0.5 * LOG2_E)
    k = rope(z[:, attn_w:2 * attn_w])
    v = z[:, 2 * attn_w:3 * attn_w]
    u = z[:, 3 * attn_w:]
    if prompt:
        qt_ref, k_ref, v_ref, u_ref, kb_ref, vt_ref, kbar_ref = outs
        qt_ref[...] = q.T
        kb_ref[...] = k.astype(BF16)
        vt = v.T.astype(BF16)
        ones = jnp.ones((BF16_SUBLANES, tm), BF16)
        vt = jnp.concatenate([piece for h in range(attn_w // head_dim)
                              for piece in (vt[h * head_dim:(h + 1) * head_dim], ones)], axis=0)
        for i in range(tm // MOBA_BLOCK):
            rows = slice(i * MOBA_BLOCK, (i + 1) * MOBA_BLOCK)
            vt_ref[i] = vt[:, rows]
            kbar_ref[i] = jnp.sum(k[rows], axis=0, keepdims=True) * (1.0 / MOBA_BLOCK)
    else:
        q_ref, k_ref, v_ref, u_ref = outs
        q_ref[...] = q
    k_ref[...] = k
    v_ref[...] = v
    u_ref[...] = u


def _rope_tables(pos, head_dim):
    inv = ROPE_THETA ** (-jnp.arange(0, head_dim, 2, dtype=F32) / head_dim)
    ang = pos.astype(F32)[:, None] * inv[None, :]
    c, s = jnp.cos(ang), jnp.sin(ang)
    reps = LANES // head_dim
    return jnp.tile(jnp.concatenate([c, c], 1), (1, reps)), jnp.tile(jnp.concatenate([-s, s], 1), (1, reps))


def _in_proj(x, g, w_bf, pos, attn_w, head_dim, prompt):
    rows, d = x.shape
    tm = min(IN_PROJ_ROWS, rows)
    cos, sin = _rope_tables(pos, head_dim)
    row_blk = lambda w: pl.BlockSpec((tm, w), lambda i: (i, 0))
    full = lambda a: pl.BlockSpec(a.shape, lambda i: (0,) * a.ndim)
    nat = jax.ShapeDtypeStruct((rows, attn_w), F32)
    if prompt:
        nblk, bpt = rows // MOBA_BLOCK, tm // MOBA_BLOCK
        vt_rows = attn_w // head_dim * (head_dim + BF16_SUBLANES)
        out_shape = [jax.ShapeDtypeStruct((attn_w, rows), F32), nat, nat, nat,
                     jax.ShapeDtypeStruct((rows, attn_w), BF16),
                     jax.ShapeDtypeStruct((nblk, vt_rows, MOBA_BLOCK), BF16),
                     jax.ShapeDtypeStruct((nblk, 1, attn_w), F32)]
        out_specs = [pl.BlockSpec((attn_w, tm), lambda i: (0, i)), row_blk(attn_w), row_blk(attn_w),
                     row_blk(attn_w), row_blk(attn_w),
                     pl.BlockSpec((bpt, vt_rows, MOBA_BLOCK), lambda i: (i, 0, 0)),
                     pl.BlockSpec((bpt, 1, attn_w), lambda i: (i, 0, 0))]
    else:
        out_shape = [nat, nat, nat, nat]
        out_specs = [row_blk(attn_w)] * 4
    return pl.pallas_call(
        functools.partial(_in_proj_kernel, attn_w=attn_w, head_dim=head_dim, prompt=prompt),
        grid=(rows // tm,),
        in_specs=[row_blk(d), full(g), full(w_bf), row_blk(LANES), row_blk(LANES)],
        out_specs=out_specs, out_shape=out_shape,
        compiler_params=_params("parallel"),
        name="in_proj_prompt" if prompt else "in_proj_sample",
    )(x, g, w_bf, cos, sin)


def _moba_prompt_kernel(qt_ref, kb_ref, vt_ref, kbar_ref, o_ref,
                        qm_ref, bias_ref, m_ref, acc_ref, sa_ref, sb_ref, *, head_dim, heads):
    t = pl.program_id(1)
    v_rows = head_dim + BF16_SUBLANES
    nblk = kbar_ref.shape[0]
    tq = qt_ref.shape[1]
    qt = qt_ref[...]
    row_head = lax.broadcasted_iota(jnp.int32, qt.shape, 0) // head_dim
    blk_idx = lax.broadcasted_iota(jnp.int32, (nblk, tq), 0)
    past = blk_idx < t
    key_pos = lax.broadcasted_iota(jnp.int32, (MOBA_BLOCK, tq), 0)
    qry_pos = lax.broadcasted_iota(jnp.int32, (MOBA_BLOCK, tq), 1)
    causal = key_pos <= qry_pos
    kbar = kbar_ref[...]
    k_own = kb_ref[t]
    topk = min(MOBA_TOPK, nblk)

    def v_t(j, h):
        return vt_ref[j, h * v_rows:(h + 1) * v_rows, :]

    def scores(j, dst_ref):
        kj = kb_ref[j]
        for h in range(heads):
            dst_ref[h] = jnp.dot(kj, qm_ref[h], preferred_element_type=F32)

    def absorb(j, src_ref):
        for h in range(heads):
            s = src_ref[h]
            bias = bias_ref[h, pl.ds(j, 1), :]
            m_prev = m_ref[h]
            m_new = jnp.maximum(m_prev, jnp.max(s, axis=0, keepdims=True) + bias)
            alpha = jnp.exp2(m_prev - m_new)
            p = jnp.exp2(s - (m_new - bias))
            acc_ref[h] = alpha * acc_ref[h] + jnp.dot(v_t(j, h), p.astype(BF16),
                                                      preferred_element_type=F32)
            m_ref[h] = m_new

    for h in range(heads):
        qm_ref[h] = jnp.where(row_head == h, qt, 0.0).astype(BF16)
    scores(0, sa_ref)
    for h in range(heads):
        qm = jnp.where(row_head == h, qt, 0.0)
        sc = jnp.dot(kbar, qm, precision=lax.Precision.HIGHEST, preferred_element_type=F32)
        picked = _top_k_mask(jnp.where(past, sc, -jnp.inf), blk_idx.astype(F32), topk, 0, nblk)
        bias_ref[h] = jnp.where(picked & past, 0.0, MASKED)
        s = jnp.dot(k_own, qm_ref[h], preferred_element_type=F32)
        s = jnp.where(causal, s, MASKED)
        m0 = jnp.max(s, axis=0, keepdims=True)
        p = jnp.exp2(s - m0)
        m_ref[h] = m0
        acc_ref[h] = jnp.dot(v_t(t, h), p.astype(BF16), preferred_element_type=F32)

    def pair(i, carry):
        j = 2 * i
        scores(j + 1, sb_ref)
        absorb(j, sa_ref)
        scores(j + 2, sa_ref)
        absorb(j + 1, sb_ref)
        return carry

    lax.fori_loop(0, t // 2, pair, 0)

    @pl.when(t % 2 == 1)
    def _():
        absorb(t - 1, sa_ref)

    o = jnp.concatenate([acc_ref[h, :head_dim, :] / acc_ref[h, head_dim:head_dim + 1, :]
                         for h in range(heads)], axis=0)
    o_ref[...] = o.T.astype(o_ref.dtype)


def _moba_prompt(qt, kb, vt, kbar, head_dim):
    attn_w, s_len = qt.shape
    nblk = s_len // MOBA_BLOCK
    gw = 2 * LANES
    heads = gw // head_dim
    kb3 = kb.reshape(nblk, MOBA_BLOCK, attn_w)
    v_rows = head_dim + BF16_SUBLANES
    return pl.pallas_call(
        functools.partial(_moba_prompt_kernel, head_dim=head_dim, heads=heads),
        grid=(attn_w // gw, nblk),
        in_specs=[pl.BlockSpec((gw, MOBA_BLOCK), lambda g, t: (g, t)),
                  pl.BlockSpec((nblk, MOBA_BLOCK, gw), lambda g, t: (0, 0, g)),
                  pl.BlockSpec((nblk, heads * v_rows, MOBA_BLOCK), lambda g, t: (0, g, 0)),
                  pl.BlockSpec((nblk, gw), lambda g, t: (0, g))],
        out_specs=pl.BlockSpec((MOBA_BLOCK, gw), lambda g, t: (t, g)),
        out_shape=jax.ShapeDtypeStruct((s_len, attn_w), BF16),
        scratch_shapes=[pltpu.VMEM((heads, gw, MOBA_BLOCK), BF16),
                        pltpu.VMEM((heads, nblk, MOBA_BLOCK), F32),
                        pltpu.VMEM((heads, 1, MOBA_BLOCK), F32),
                        pltpu.VMEM((heads, v_rows, MOBA_BLOCK), F32),
                        pltpu.VMEM((heads, MOBA_BLOCK, MOBA_BLOCK), F32),
                        pltpu.VMEM((heads, MOBA_BLOCK, MOBA_BLOCK), F32)],
        compiler_params=_params("parallel", "arbitrary"),
        name="moba_prompt",
    )(qt, kb3, vt, kbar)


def _moba_sample_kernel(pt_ref, q_ref, kn_ref, vn_ref, *rest, pps, n_heads, head_dim):
    del pt_ref
    k_refs, v_refs = rest[:pps], rest[pps:2 * pps]
    o_ref, s_ref, p_ref, kbar_ref, acc_ref, qbd_ref, qbf_ref, l_ref, pown_ref = rest[2 * pps:]
    b, c = pl.program_id(0), pl.program_id(1)
    n_seq, n_chunks = pl.num_programs(0) - 1, s_ref.shape[0]
    lq, width = q_ref.shape
    page = k_refs[0].shape[1]
    ncol = n_heads * lq
    ppb = MOBA_BLOCK // page
    bps = pps // ppb
    nblk = n_chunks * bps

    def block_t(refs, jj):
        return jnp.concatenate([refs[jj * ppb + i][...] for i in range(ppb)], axis=1)

    def cols(jj):
        return slice(jj * MOBA_BLOCK, (jj + 1) * MOBA_BLOCK)

    @pl.when(b > 0)
    def _apply_values():
        acc = jnp.zeros((ncol, width), F32)
        for jj in range(bps):
            acc += lax.dot_general(p_ref[c, :, cols(jj)], block_t(v_refs, jj).astype(BF16), _NT,
                                   preferred_element_type=F32)

        @pl.when(c == 0)
        def _():
            acc_ref[...] = acc

        @pl.when(c > 0)
        def _():
            acc_ref[...] += acc

        @pl.when(c == n_chunks - 1)
        def _finish():
            o = acc_ref[...]
            vn, pown = vn_ref[...], pown_ref[...]
            for j in range(lq):
                o = o + pown[:, j:j + 1] * vn[j:j + 1, :]
            o = o / l_ref[...]
            lane_head = lax.broadcasted_iota(jnp.int32, (lq, width), 1) // head_dim
            out = jnp.zeros((lq, width), F32)
            for h in range(n_heads):
                out = jnp.where(lane_head == h, o[h * lq:(h + 1) * lq, :], out)
            o_ref[...] = out.astype(o_ref.dtype)

    @pl.when(b < n_seq)
    def _score_keys():
        @pl.when(c == 0)
        def _():
            q_rep = jnp.concatenate([q_ref[...]] * n_heads, axis=0)
            row_head = lax.broadcasted_iota(jnp.int32, (ncol, width), 0) // lq
            lane_head = lax.broadcasted_iota(jnp.int32, (ncol, width), 1) // head_dim
            qbd = jnp.where(row_head == lane_head, q_rep, 0.0)
            qbf_ref[...] = qbd
            qbd_ref[...] = qbd.astype(BF16)

        qbd = qbd_ref[...]
        for jj in range(bps):
            kt = block_t(k_refs, jj)
            kbar_ref[c, :, jj:jj + 1] = jnp.sum(kt, axis=1, keepdims=True) * (1.0 / MOBA_BLOCK)
            s_ref[c, :, cols(jj)] = jnp.dot(qbd, kt.astype(BF16), preferred_element_type=F32)

        @pl.when(c == n_chunks - 1)
        def _select_and_normalise():
            kbar = jnp.concatenate([kbar_ref[i] for i in range(n_chunks)], axis=1)
            sc = jnp.dot(qbf_ref[...], kbar, precision=lax.Precision.HIGHEST,
                         preferred_element_type=F32)
            blk_idx = lax.broadcasted_iota(jnp.int32, (ncol, nblk), 1)
            picked = _top_k_mask(sc, blk_idx.astype(F32), min(MOBA_TOPK, nblk), 1, nblk)
            bias = jnp.where(picked, 0.0, MASKED)
            s_own = lax.dot_general(qbd, kn_ref[...].astype(BF16), _NT, preferred_element_type=F32)
            qry = lax.broadcasted_iota(jnp.int32, (ncol, lq), 0) % lq
            key = lax.broadcasted_iota(jnp.int32, (ncol, lq), 1)
            s_own = jnp.where(key <= qry, s_own, MASKED)

            def lane_fold(x, op):
                out = x[:, :LANES]
                for i in range(1, MOBA_BLOCK // LANES):
                    out = op(out, x[:, i * LANES:(i + 1) * LANES])
                return out

            m_acc = jnp.full((ncol, LANES), MASKED, F32)
            for j in range(nblk):
                sj = s_ref[j // bps, :, cols(j % bps)] + bias[:, j:j + 1]
                m_acc = jnp.maximum(m_acc, lane_fold(sj, jnp.maximum))
            m = jnp.maximum(jnp.max(s_own, axis=1, keepdims=True), jnp.max(m_acc, axis=1, keepdims=True))
            p_own = jnp.exp2(s_own - m)
            pown_ref[...] = p_own
            l_acc = jnp.zeros((ncol, LANES), F32)
            for j in range(nblk):
                p = jnp.exp2(s_ref[j // bps, :, cols(j % bps)] + (bias[:, j:j + 1] - m))
                l_acc = l_acc + lane_fold(p, jnp.add)
                p_ref[j // bps, :, cols(j % bps)] = p.astype(BF16)
            l_ref[...] = jnp.sum(p_own, axis=1, keepdims=True) + jnp.sum(l_acc, axis=1, keepdims=True)


def _pages_t(cache):
    n_phys, _, page, n_heads, head_dim = cache.shape
    return jnp.transpose(cache[:, 0], (0, 2, 3, 1)).reshape(n_phys, n_heads * head_dim, page)


def _moba_sample(q, k_new, v_new, cache_kt, cache_vt, page_table, n_heads, head_dim):
    n_seq, lq, width = q.shape
    page = cache_kt.shape[2]
    n_pages = page_table.shape[1]
    pps = min(SAMPLE_PAGES_PER_STEP, n_pages)
    n_chunks = n_pages // pps
    bps = pps * page // MOBA_BLOCK
    ncol = n_heads * lq
    last = n_seq - 1

    def seq_spec(idx):
        return pl.BlockSpec((None, lq, width), lambda b, c, pt: (idx(b), 0, 0))

    def page_spec(idx, i):
        return pl.BlockSpec((None, width, page),
                            lambda b, c, pt: (pt[idx(b) * n_pages + c * pps + i], 0, 0))

    cur = lambda b: jnp.minimum(b, last)
    prev = lambda b: jnp.maximum(b - 1, 0)
    grid_spec = pltpu.PrefetchScalarGridSpec(
        num_scalar_prefetch=1,
        grid=(n_seq + 1, n_chunks),
        in_specs=[seq_spec(cur), seq_spec(cur), seq_spec(prev)]
        + [page_spec(cur, i) for i in range(pps)] + [page_spec(prev, i) for i in range(pps)],
        out_specs=seq_spec(prev),
        scratch_shapes=[pltpu.VMEM((n_chunks, ncol, pps * page), F32),
                        pltpu.VMEM((n_chunks, ncol, pps * page), BF16),
                        pltpu.VMEM((n_chunks, width, bps), F32),
                        pltpu.VMEM((ncol, width), F32),
                        pltpu.VMEM((ncol, width), BF16),
                        pltpu.VMEM((ncol, width), F32),
                        pltpu.VMEM((ncol, 1), F32),
                        pltpu.VMEM((ncol, lq), F32)])
    return pl.pallas_call(
        functools.partial(_moba_sample_kernel, pps=pps, n_heads=n_heads, head_dim=head_dim),
        grid_spec=grid_spec,
        out_shape=jax.ShapeDtypeStruct((n_seq, lq, width), BF16),
        compiler_params=_params("arbitrary", "arbitrary"),
        name="moba_sample",
    )(page_table.reshape(-1), q, k_new, v_new, *([cache_kt] * pps), *([cache_vt] * pps))


def _mix_out_kernel(*refs, prompt, start_pos, hist):
    if prompt:
        x_ref, a_ref, u_ref, uh_ref, wpool_ref, ps_ref, wout_ref, g_ref, x2_ref, h_ref, ext_ref = refs
        tm = u_ref.shape[0]
        t = pl.program_id(0)
        ext_ref[0:hist + 1] = jnp.where(t > 0, uh_ref[...], 0.0)
        ext_ref[hist + 1:] = u_ref[...]
        pos = t * tm + lax.broadcasted_iota(jnp.int32, (tm, 1), 0)

        def shifted(i, lanes):
            return ext_ref[pl.ds(hist + 1 - i, tm), lanes]
    else:
        (x_ref, a_ref, u_ref, uh_ref, wpool_ref, ps_ref, wout_ref, g_ref,
         x2_ref, h_ref, state_ref, ext_ref) = refs
        tb, lq = u_ref.shape[0], u_ref.shape[1]
        tm = tb * lq
        ext_ref[:, 1:hist + 1, :] = uh_ref[...]
        ext_ref[:, hist + 1:, :] = u_ref[...]
        state_ref[...] = ext_ref[:, lq + 1:, :]
        pos = start_pos + lax.broadcasted_iota(jnp.int32, (tm, 1), 0) % lq

        def shifted(i, lanes):
            return ext_ref[:, pl.ds(hist + 1 - i, lq), lanes].reshape(tm, lanes.stop - lanes.start)

    gw = wpool_ref.shape[1]
    ys = []
    for g, w in enumerate(POOL_WINDOWS):
        lanes = slice(g * gw, (g + 1) * gw)
        cur = shifted(0, lanes)
        total = cur
        for i in range(1, w):
            total = total + shifted(i, lanes)
        cnt = jnp.minimum(w, pos + 1).astype(F32)
        d = total / cnt - cur
        ys.append(jnp.dot(d.astype(BF16), wpool_ref[g], preferred_element_type=F32))
    pooled = jnp.concatenate(ys, axis=1) * ps_ref[...]
    mixed = jnp.concatenate([a_ref[...], pooled.astype(BF16)], axis=1)
    x2 = x_ref[...] + jnp.dot(mixed, wout_ref[...], preferred_element_type=F32)
    x2_ref[...] = x2
    h_ref[...] = _rmsnorm(x2, g_ref[...]).astype(BF16)


def _mix_out(x, a, u, u_hist, wpool_bf, pool_scale, wout_bf, g_ffn, prompt, start_pos):
    rows, d = x.shape
    pool_w = a.shape[1]
    hist = max(POOL_WINDOWS) - 1
    full = lambda arr: pl.BlockSpec(arr.shape, lambda i: (0,) * arr.ndim)
    row_blk = lambda w: pl.BlockSpec((tm, w), lambda i: (i, 0))
    out_shape = [jax.ShapeDtypeStruct((rows, d), F32), jax.ShapeDtypeStruct((rows, d), BF16)]
    if prompt:
        tm = min(MIX_ROWS, rows)
        halo = hist + 1
        u_specs = [row_blk(pool_w),
                   pl.BlockSpec((halo, pool_w), lambda i: (jnp.maximum(i * (tm // halo) - 1, 0), 0))]
        u_args = (u, u)
        out_specs = [row_blk(d), row_blk(d)]
        scratch = [pltpu.VMEM((tm + halo, pool_w), F32)]
        grid = (rows // tm,)
    else:
        n_seq, lq = u.shape[0], u.shape[1]
        tm = rows
        u_specs = [pl.BlockSpec((n_seq, lq, pool_w), lambda i: (0, 0, 0)),
                   pl.BlockSpec((n_seq, hist, pool_w), lambda i: (0, 0, 0))]
        u_args = (u, u_hist)
        out_shape.append(jax.ShapeDtypeStruct((n_seq, hist, pool_w), F32))
        out_specs = [row_blk(d), row_blk(d), pl.BlockSpec((n_seq, hist, pool_w), lambda i: (0, 0, 0))]
        scratch = [pltpu.VMEM((n_seq, hist + 1 + lq, pool_w), F32)]
        grid = (1,)
    return pl.pallas_call(
        functools.partial(_mix_out_kernel, prompt=prompt, start_pos=start_pos, hist=hist),
        grid=grid,
        in_specs=[row_blk(d), row_blk(a.shape[1])] + u_specs
        + [full(wpool_bf), full(pool_scale), full(wout_bf), full(g_ffn)],
        out_specs=out_specs, out_shape=out_shape, scratch_shapes=scratch,
        compiler_params=_params("parallel"),
        name="mix_out_prompt" if prompt else "mix_out_sample",
    )(x, a, *u_args, wpool_bf, pool_scale, wout_bf, g_ffn)


def _ffn_tail_kernel(*refs, prompt):
    if prompt:
        (h_ref, hh_ref, x2_ref, p_ref, wg_ref, wv_ref, cwg_ref, cwv_ref, cbg_ref, cbv_ref, wd_ref,
         wpg_ref, wpp_ref, gf_ref, y_ref, sg_ref, sv_ref, hext_ref, eg_ref, ev_ref, acc_ref) = refs
    else:
        (h_ref, hg_ref, hv_ref, x2_ref, p_ref, wg_ref, wv_ref, cwg_ref, cwv_ref, cbg_ref, cbv_ref,
         wd_ref, wpg_ref, wpp_ref, gf_ref, y_ref, sg_ref, sv_ref, eg_ref, ev_ref, acc_ref) = refs
    t, c = pl.program_id(0), pl.program_id(1)
    tm = h_ref.shape[0]
    keep = CONV_W - 1
    if prompt:
        halo = hh_ref.shape[0]

        @pl.when(c == 0)
        def _():
            hext_ref[0:halo] = jnp.where(t > 0, hh_ref[...], jnp.zeros(hh_ref.shape, hh_ref.dtype))
            hext_ref[halo:] = h_ref[...]

        hx = hext_ref[...]
        eg_ref[...] = jnp.dot(hx, wg_ref[...], preferred_element_type=F32)
        ev_ref[...] = jnp.dot(hx, wv_ref[...], preferred_element_type=F32)
        sg_ref[...] = eg_ref[tm + halo - 8:, :]
        sv_ref[...] = ev_ref[tm + halo - 8:, :]

        def delayed(ref, i):
            return ref[pl.ds(halo - i, tm), :]
    else:
        tb, lq = eg_ref.shape[0], eg_ref.shape[1] - 8
        fc = eg_ref.shape[2]
        hx = h_ref[...]
        eg_ref[:, 8:, :] = jnp.dot(hx, wg_ref[...], preferred_element_type=F32).reshape(tb, lq, fc)
        ev_ref[:, 8:, :] = jnp.dot(hx, wv_ref[...], preferred_element_type=F32).reshape(tb, lq, fc)
        eg_ref[:, 8 - keep:8, :] = hg_ref[...]
        ev_ref[:, 8 - keep:8, :] = hv_ref[...]
        sg_ref[...] = eg_ref[:, 8 + lq - keep:, :]
        sv_ref[...] = ev_ref[:, 8 + lq - keep:, :]

        def delayed(ref, i):
            return ref[:, pl.ds(8 - i, lq), :].reshape(tm, fc)

    def conv(ref, cw_ref, cb_ref):
        cw = cw_ref[...]
        out = cb_ref[...]
        for i in range(CONV_W):
            out = out + delayed(ref, CONV_W - 1 - i) * cw[i:i + 1, :]
        return out

    gate = conv(eg_ref, cwg_ref, cbg_ref)
    val = conv(ev_ref, cwv_ref, cbv_ref)
    act = (gate * jax.nn.sigmoid(gate) * val).astype(BF16)
    part = jnp.dot(act, wd_ref[...], preferred_element_type=F32)

    @pl.when(c == 0)
    def _():
        acc_ref[...] = part

    @pl.when(c > 0)
    def _():
        acc_ref[...] += part

    @pl.when(c == pl.num_programs(1) - 1)
    def _():
        x3 = x2_ref[...] + acc_ref[...]
        gate_p = jax.nn.sigmoid(jnp.dot(x3.astype(BF16), wpg_ref[...], preferred_element_type=F32))
        emb = jnp.dot(p_ref[...].astype(BF16), wpp_ref[...], preferred_element_type=F32)
        y_ref[...] = _rmsnorm(x3 + gate_p * emb, gf_ref[...])


def _ffn_tail(h, x2, p, state, wup_bf, conv_w, conv_b, wdown_bf, wpg_bf, wpp_bf, g_final, prompt):
    rows, d = x2.shape
    d_ff = wdown_bf.shape[0]
    fc = FFN_CHUNK
    n_fc = d_ff // fc
    p_dim = p.shape[1]
    tm = min(FFN_ROWS, rows)
    n_t = rows // tm
    row_blk = lambda w: pl.BlockSpec((tm, w), lambda t, c: (t, 0))
    full = lambda arr: pl.BlockSpec(arr.shape, lambda t, c: (0,) * arr.ndim)
    gate_cols = lambda r: pl.BlockSpec((r, fc), lambda t, c: (0, c))
    val_cols = lambda r: pl.BlockSpec((r, fc), lambda t, c: (0, c + n_fc))
    weights = [gate_cols(d), val_cols(d), gate_cols(CONV_W), val_cols(CONV_W), gate_cols(1), val_cols(1),
               pl.BlockSpec((fc, d), lambda t, c: (c, 0)), full(wpg_bf), full(wpp_bf), full(g_final)]
    weight_args = (wup_bf, wup_bf, conv_w, conv_w, conv_b, conv_b, wdown_bf, wpg_bf, wpp_bf, g_final)
    y_shape = jax.ShapeDtypeStruct((rows, d), F32)
    if prompt:
        halo = BF16_SUBLANES
        acts = [row_blk(d),
                pl.BlockSpec((halo, d), lambda t, c: (jnp.maximum(t * (tm // halo) - 1, 0), 0)),
                row_blk(d), row_blk(p_dim)]
        act_args = (h, h, x2, p)
        st_shape = jax.ShapeDtypeStruct((n_t * 8, d_ff), F32)
        st_spec = pl.BlockSpec((8, fc), lambda t, c: (t, c))
        scratch = [pltpu.VMEM((tm + halo, d), BF16), pltpu.VMEM((tm + halo, fc), F32),
                   pltpu.VMEM((tm + halo, fc), F32), pltpu.VMEM((tm, d), F32)]
    else:
        n_seq, keep = state.shape[0], state.shape[1]
        lq = rows // n_seq
        hist_g = pl.BlockSpec((n_seq, keep, fc), lambda t, c: (0, 0, c))
        hist_v = pl.BlockSpec((n_seq, keep, fc), lambda t, c: (0, 0, c + n_fc))
        acts = [row_blk(d), hist_g, hist_v, row_blk(d), row_blk(p_dim)]
        act_args = (h, state, state, x2, p)
        st_shape = jax.ShapeDtypeStruct((n_seq, keep, d_ff), F32)
        st_spec = pl.BlockSpec((n_seq, keep, fc), lambda t, c: (0, 0, c))
        scratch = [pltpu.VMEM((n_seq, 8 + lq, fc), F32), pltpu.VMEM((n_seq, 8 + lq, fc), F32),
                   pltpu.VMEM((tm, d), F32)]
    return pl.pallas_call(
        functools.partial(_ffn_tail_kernel, prompt=prompt),
        grid=(n_t, n_fc),
        in_specs=acts + weights,
        out_specs=[row_blk(d), st_spec, st_spec],
        out_shape=[y_shape, st_shape, st_shape],
        scratch_shapes=scratch,
        compiler_params=_params("parallel", "arbitrary"),
        name="ffn_tail_prompt" if prompt else "ffn_tail_sample",
    )(*act_args, *weight_args)


def kernel(x_prompt, x_sample, cache_k, cache_v, state_pool, state_conv, page_table, p_prompt, p_sample,
           g_mix, w_in, w_out, w_pool, pool_scale, g_ffn, w_up, conv_w, conv_b, w_down, w_pg, w_pp, g_final):
    b_p, s_len, d = x_prompt.shape
    n_seq, lq, _ = x_sample.shape
    n_phys, depth, page, n_heads, head_dim = cache_k.shape
    attn_w = n_heads * head_dim
    pool_w = d - attn_w
    past_len = page_table.shape[1] * page
    hist = state_pool.shape[2]
    keep = state_conv.shape[2]
    d_ff = w_down.shape[1]
    assert b_p == 1 and depth == 1
    assert past_len % MOBA_BLOCK == 0 and MOBA_BLOCK % page == 0 and s_len % MOBA_BLOCK == 0
    assert lq == 8 and hist == max(POOL_WINDOWS) - 1 and keep == CONV_W - 1
    assert d_ff % FFN_CHUNK == 0

    i = 0
    row = lambda v: v.reshape(1, -1)
    w_in_bf, w_out_bf, w_pool_bf = w_in[i].astype(BF16), w_out[i].astype(BF16), w_pool[i].astype(BF16)
    w_up_bf, w_down_bf = w_up[i].astype(BF16), w_down[i].astype(BF16)
    w_pg_bf, w_pp_bf = w_pg[i].astype(BF16), w_pp[i].astype(BF16)

    xp = x_prompt.reshape(s_len, d)
    qt, k_p, v_p, u_p, kb, vt, kbar = _in_proj(xp, row(g_mix[i]), w_in_bf, jnp.arange(s_len),
                                               attn_w, head_dim, True)
    a_p = _moba_prompt(qt, kb, vt, kbar.reshape(-1, attn_w), head_dim)
    x2_p, h_p = _mix_out(xp, a_p, u_p, None, w_pool_bf, row(pool_scale[i]), w_out_bf, row(g_ffn[i]), True, 0)
    y_p, cg_p, cv_p = _ffn_tail(h_p, x2_p, p_prompt[i].reshape(s_len, -1), None, w_up_bf, conv_w[i],
                                row(conv_b[i]), w_down_bf, w_pg_bf, w_pp_bf, row(g_final), True)

    rows_s = n_seq * lq
    xs = x_sample.reshape(rows_s, d)
    pos_s = past_len + jnp.arange(rows_s) % lq
    q_s, k_s, v_s, u_s = _in_proj(xs, row(g_mix[i]), w_in_bf, pos_s, attn_w, head_dim, False)
    seq3 = lambda v: v.reshape(n_seq, lq, -1)
    a_s = _moba_sample(seq3(q_s), seq3(k_s), seq3(v_s), _pages_t(cache_k), _pages_t(cache_v), page_table,
                       n_heads, head_dim)
    x2_s, h_s, pool_s = _mix_out(xs, a_s.reshape(rows_s, attn_w), seq3(u_s), state_pool[:, i], w_pool_bf,
                                 row(pool_scale[i]), w_out_bf, row(g_ffn[i]), False, past_len)
    y_s, cg_s, cv_s = _ffn_tail(h_s, x2_s, p_sample[i].reshape(rows_s, -1), state_conv[:, i], w_up_bf,
                                conv_w[i], row(conv_b[i]), w_down_bf, w_pg_bf, w_pp_bf, row(g_final), False)

    heads5 = lambda v, n, l: v.reshape(n, 1, l, n_heads, head_dim)
    conv_p = jnp.concatenate([cg_p[-keep:], cv_p[-keep:]], axis=1).reshape(1, 1, keep, 2 * d_ff)
    conv_s = jnp.concatenate([cg_s, cv_s], axis=2).reshape(n_seq, 1, keep, 2 * d_ff)
    return (y_p.reshape(1, s_len, d), y_s.reshape(n_seq, lq, d),
            heads5(k_p, 1, s_len), heads5(v_p, 1, s_len), heads5(k_s, n_seq, lq), heads5(v_s, n_seq, lq),
            u_p[-hist:].reshape(1, 1, hist, pool_w), pool_s.reshape(n_seq, 1, hist, pool_w),
            conv_p, conv_s)
```

```python
import functools
import math

import jax
import jax.numpy as jnp
from jax import lax
from jax.experimental import pallas as pl
from jax.experimental.pallas import tpu as pltpu

F32 = jnp.float32
BF16 = jnp.bfloat16

MOBA_BLOCK = 256
MOBA_TOPK = 3
ROPE_THETA = 10000.0
POOL_WINDOWS = (2, 4, 8, 16)
CONV_W = 3
EPS = 1e-6
MASKED = -1e30
LOG2_E = math.log2(math.e)

LANES = 128
BF16_SUBLANES = 16
VMEM_LIMIT = 56 * 1024 * 1024

IN_PROJ_ROWS = 512
MIX_ROWS = 512
FFN_ROWS = 1024
FFN_CHUNK = 256
FFN_ROW_SPLITS = 4
SAMPLE_PAGES_PER_STEP = 32

_NT = (((1,), (1,)), ((), ()))


def _params(*sem):
    return pltpu.CompilerParams(dimension_semantics=sem, vmem_limit_bytes=VMEM_LIMIT)


def _rmsnorm(x, g):
    return x * lax.rsqrt(jnp.mean(x * x, axis=-1, keepdims=True) + EPS) * g


def _top_k_mask(s, idx, k, axis, n):
    picked = jnp.zeros(s.shape, jnp.bool_)
    for _ in range(k):
        mx = jnp.max(s, axis=axis, keepdims=True)
        first = jnp.min(jnp.where(s == mx, idx, float(n)), axis=axis, keepdims=True)
        hit = idx == first
        picked = picked | hit
        s = jnp.where(hit, -jnp.inf, s)
    return picked


def _in_proj_kernel(x_ref, g_ref, w_ref, cos_ref, sin_ref, *outs, attn_w, head_dim, prompt):
    tm = x_ref.shape[0]
    xn = _rmsnorm(x_ref[...], g_ref[...])
    z = jnp.dot(xn.astype(BF16), w_ref[...], preferred_element_type=F32)
    reps = attn_w // LANES
    cos = jnp.tile(cos_ref[...], (1, reps))
    sin = jnp.tile(sin_ref[...], (1, reps))
    half = head_dim // 2
    lane = lax.broadcasted_iota(jnp.int32, (tm, attn_w), 1)
    lower = (lane % head_dim) < half

    def rope(t):
        rot = jnp.where(lower, pltpu.roll(t, attn_w - half, 1), pltpu.roll(t, half, 1))
        return t * cos + rot * sin

    q = rope(z[:, :attn_w]) * (head_dim ** -0.5 * LOG2_E)
    k = rope(z[:, attn_w:2 * attn_w])
    v = z[:, 2 * attn_w:3 * attn_w]
    u = z[:, 3 * attn_w:]
    if prompt:
        qt_ref, k_ref, v_ref, u_ref, kb_ref, vt_ref, kbar_ref = outs
        qt_ref[...] = q.T
        kb_ref[...] = k.astype(BF16)
        vt = v.T.astype(BF16)
        ones = jnp.ones((BF16_SUBLANES, tm), BF16)
        vt = jnp.concatenate([piece for h in range(attn_w // head_dim)
                              for piece in (vt[h * head_dim:(h + 1) * head_dim], ones)], axis=0)
        for i in range(tm // MOBA_BLOCK):
            rows = slice(i * MOBA_BLOCK, (i + 1) * MOBA_BLOCK)
            vt_ref[i] = vt[:, rows]
            kbar_ref[i] = jnp.sum(k[rows], axis=0, keepdims=True) * (1.0 / MOBA_BLOCK)
    else:
        q_ref, k_ref, v_ref, u_ref = outs
        q_ref[...] = q
    k_ref[...] = k
    v_ref[...] = v
    u_ref[...] = u


def _rope_tables(pos, head_dim):
    inv = ROPE_THETA ** (-jnp.arange(0, head_dim, 2, dtype=F32) / head_dim)
    ang = pos.astype(F32)[:, None] * inv[None, :]
    c, s = jnp.cos(ang), jnp.sin(ang)
    reps = LANES // head_dim
    return jnp.tile(jnp.concatenate([c, c], 1), (1, reps)), jnp.tile(jnp.concatenate([-s, s], 1), (1, reps))


def _in_proj(x, g, w_bf, pos, attn_w, head_dim, prompt):
    rows, d = x.shape
    tm = min(IN_PROJ_ROWS, rows)
    cos, sin = _rope_tables(pos, head_dim)
    row_blk = lambda w: pl.BlockSpec((tm, w), lambda i: (i, 0))
    full = lambda a: pl.BlockSpec(a.shape, lambda i: (0,) * a.ndim)
    nat = jax.ShapeDtypeStruct((rows, attn_w), F32)
    if prompt:
        nblk, bpt = rows // MOBA_BLOCK, tm // MOBA_BLOCK
        vt_rows = attn_w // head_dim * (head_dim + BF16_SUBLANES)
        out_shape = [jax.ShapeDtypeStruct((attn_w, rows), F32), nat, nat, nat,
                     jax.ShapeDtypeStruct((rows, attn_w), BF16),
                     jax.ShapeDtypeStruct((nblk, vt_rows, MOBA_BLOCK), BF16),
                     jax.ShapeDtypeStruct((nblk, 1, attn_w), F32)]
        out_specs = [pl.BlockSpec((attn_w, tm), lambda i: (0, i)), row_blk(attn_w), row_blk(attn_w),
                     row_blk(attn_w), row_blk(attn_w),
                     pl.BlockSpec((bpt, vt_rows, MOBA_BLOCK), lambda i: (i, 0, 0)),
                     pl.BlockSpec((bpt, 1, attn_w), lambda i: (i, 0, 0))]
    else:
        out_shape = [nat, nat, nat, nat]
        out_specs = [row_blk(attn_w)] * 4
    return pl.pallas_call(
        functools.partial(_in_proj_kernel, attn_w=attn_w, head_dim=head_dim, prompt=prompt),
        grid=(rows // tm,),
        in_specs=[row_blk(d), full(g), full(w_bf), row_blk(LANES), row_blk(LANES)],
        out_specs=out_specs, out_shape=out_shape,
        compiler_params=_params("parallel"),
        name="in_proj_prompt" if prompt else "in_proj_sample",
    )(x, g, w_bf, cos, sin)


def _moba_prompt_kernel(qt_ref, kb_ref, vt_ref, kbar_ref, o_ref,
                        qm_ref, bias_ref, m_ref, acc_ref, sa_ref, sb_ref, ca_ref, cb_ref,
                        *, head_dim, heads):
    t = pl.program_id(1)
    v_rows = head_dim + BF16_SUBLANES
    nblk = kbar_ref.shape[0]
    tq = qt_ref.shape[1]
    qt = qt_ref[...]
    row_head = lax.broadcasted_iota(jnp.int32, qt.shape, 0) // head_dim
    blk_idx = lax.broadcasted_iota(jnp.int32, (nblk, tq), 0)
    past = blk_idx < t
    key_pos = lax.broadcasted_iota(jnp.int32, (MOBA_BLOCK, tq), 0)
    qry_pos = lax.broadcasted_iota(jnp.int32, (MOBA_BLOCK, tq), 1)
    causal = key_pos <= qry_pos
    kbar = kbar_ref[...]
    k_own = kb_ref[t]
    topk = min(MOBA_TOPK, nblk)

    def v_t(j, h):
        return vt_ref[j, h * v_rows:(h + 1) * v_rows, :]

    buf_a, buf_b = (sa_ref, ca_ref), (sb_ref, cb_ref)

    def scores(j, buf):
        s_ref, c_ref = buf
        kj = kb_ref[j]
        for h in range(heads):
            s = jnp.dot(kj, qm_ref[h], preferred_element_type=F32)
            s_ref[h] = s
            c_ref[h] = jnp.max(s, axis=0, keepdims=True)

    def absorb(j, buf):
        s_ref, c_ref = buf
        for h in range(heads):
            bias = bias_ref[h, pl.ds(j, 1), :]
            m_prev = m_ref[h]
            m_new = jnp.maximum(m_prev, c_ref[h] + bias)
            alpha = jnp.exp2(m_prev - m_new)
            p = jnp.exp2(s_ref[h] - (m_new - bias))
            acc_ref[h] = alpha * acc_ref[h] + jnp.dot(v_t(j, h), p.astype(BF16),
                                                      preferred_element_type=F32)
            m_ref[h] = m_new

    for h in range(heads):
        qm_ref[h] = jnp.where(row_head == h, qt, 0.0).astype(BF16)
    scores(0, buf_a)
    for h in range(heads):
        qm = jnp.where(row_head == h, qt, 0.0)
        sc = jnp.dot(kbar, qm, precision=lax.Precision.HIGHEST, preferred_element_type=F32)
        picked = _top_k_mask(jnp.where(past, sc, -jnp.inf), blk_idx.astype(F32), topk, 0, nblk)
        bias_ref[h] = jnp.where(picked & past, 0.0, MASKED)
        s = jnp.dot(k_own, qm_ref[h], preferred_element_type=F32)
        s = jnp.where(causal, s, MASKED)
        m0 = jnp.max(s, axis=0, keepdims=True)
        p = jnp.exp2(s - m0)
        m_ref[h] = m0
        acc_ref[h] = jnp.dot(v_t(t, h), p.astype(BF16), preferred_element_type=F32)

    def pair(j):
        scores(j + 1, buf_b)
        absorb(j, buf_a)
        scores(j + 2, buf_a)
        absorb(j + 1, buf_b)

    def quad(i, carry):
        pair(4 * i)
        pair(4 * i + 2)
        return carry

    lax.fori_loop(0, t // 4, quad, 0)

    @pl.when(t % 4 >= 2)
    def _():
        pair(t // 4 * 4)

    @pl.when(t % 2 == 1)
    def _():
        absorb(t - 1, buf_a)

    o = jnp.concatenate([acc_ref[h, :head_dim, :] / acc_ref[h, head_dim:head_dim + 1, :]
                         for h in range(heads)], axis=0)
    o_ref[...] = o.T.astype(o_ref.dtype)


def _moba_prompt(qt, kb, vt, kbar, head_dim):
    attn_w, s_len = qt.shape
    nblk = s_len // MOBA_BLOCK
    gw = 2 * LANES
    heads = gw // head_dim
    kb3 = kb.reshape(nblk, MOBA_BLOCK, attn_w)
    v_rows = head_dim + BF16_SUBLANES
    return pl.pallas_call(
        functools.partial(_moba_prompt_kernel, head_dim=head_dim, heads=heads),
        grid=(attn_w // gw, nblk),
        in_specs=[pl.BlockSpec((gw, MOBA_BLOCK), lambda g, t: (g, t)),
                  pl.BlockSpec((nblk, MOBA_BLOCK, gw), lambda g, t: (0, 0, g)),
                  pl.BlockSpec((nblk, heads * v_rows, MOBA_BLOCK), lambda g, t: (0, g, 0)),
                  pl.BlockSpec((nblk, gw), lambda g, t: (0, g))],
        out_specs=pl.BlockSpec((MOBA_BLOCK, gw), lambda g, t: (t, g)),
        out_shape=jax.ShapeDtypeStruct((s_len, attn_w), BF16),
        scratch_shapes=[pltpu.VMEM((heads, gw, MOBA_BLOCK), BF16),
                        pltpu.VMEM((heads, nblk, MOBA_BLOCK), F32),
                        pltpu.VMEM((heads, 1, MOBA_BLOCK), F32),
                        pltpu.VMEM((heads, v_rows, MOBA_BLOCK), F32),
                        pltpu.VMEM((heads, MOBA_BLOCK, MOBA_BLOCK), F32),
                        pltpu.VMEM((heads, MOBA_BLOCK, MOBA_BLOCK), F32),
                        pltpu.VMEM((heads, 1, MOBA_BLOCK), F32),
                        pltpu.VMEM((heads, 1, MOBA_BLOCK), F32)],
        compiler_params=_params("parallel", "arbitrary"),
        name="moba_prompt",
    )(qt, kb3, vt, kbar)


def _moba_sample_kernel(pt_ref, q_ref, kn_ref, vn_ref, *rest, pps, n_heads, head_dim):
    del pt_ref
    k_refs, v_refs = rest[:pps], rest[pps:2 * pps]
    o_ref, s_ref, p_ref, kbar_ref, acc_ref, qbd_ref, qbf_ref, l_ref, pown_ref = rest[2 * pps:]
    b, c = pl.program_id(0), pl.program_id(1)
    n_seq, n_chunks = pl.num_programs(0) - 1, s_ref.shape[0]
    lq, width = q_ref.shape
    page = k_refs[0].shape[1]
    ncol = n_heads * lq
    ppb = MOBA_BLOCK // page
    bps = pps // ppb
    nblk = n_chunks * bps

    def block_t(refs, jj):
        return jnp.concatenate([refs[jj * ppb + i][...] for i in range(ppb)], axis=1)

    def cols(jj):
        return slice(jj * MOBA_BLOCK, (jj + 1) * MOBA_BLOCK)

    @pl.when(b > 0)
    def _apply_values():
        acc = jnp.zeros((ncol, width), F32)
        for jj in range(bps):
            acc += lax.dot_general(p_ref[c, :, cols(jj)], block_t(v_refs, jj).astype(BF16), _NT,
                                   preferred_element_type=F32)

        @pl.when(c == 0)
        def _():
            acc_ref[...] = acc

        @pl.when(c > 0)
        def _():
            acc_ref[...] += acc

        @pl.when(c == n_chunks - 1)
        def _finish():
            o = acc_ref[...]
            vn, pown = vn_ref[...], pown_ref[...]
            for j in range(lq):
                o = o + pown[:, j:j + 1] * vn[j:j + 1, :]
            o = o / l_ref[...]
            lane_head = lax.broadcasted_iota(jnp.int32, (lq, width), 1) // head_dim
            out = jnp.zeros((lq, width), F32)
            for h in range(n_heads):
                out = jnp.where(lane_head == h, o[h * lq:(h + 1) * lq, :], out)
            o_ref[...] = out.astype(o_ref.dtype)

    @pl.when(b < n_seq)
    def _score_keys():
        @pl.when(c == 0)
        def _():
            q_rep = jnp.concatenate([q_ref[...]] * n_heads, axis=0)
            row_head = lax.broadcasted_iota(jnp.int32, (ncol, width), 0) // lq
            lane_head = lax.broadcasted_iota(jnp.int32, (ncol, width), 1) // head_dim
            qbd = jnp.where(row_head == lane_head, q_rep, 0.0)
            qbf_ref[...] = qbd
            qbd_ref[...] = qbd.astype(BF16)

        qbd = qbd_ref[...]
        for jj in range(bps):
            kt = block_t(k_refs, jj)
            kbar_ref[c, :, jj:jj + 1] = jnp.sum(kt, axis=1, keepdims=True) * (1.0 / MOBA_BLOCK)
            s_ref[c, :, cols(jj)] = jnp.dot(qbd, kt.astype(BF16), preferred_element_type=F32)

        @pl.when(c == n_chunks - 1)
        def _select_and_normalise():
            kbar = jnp.concatenate([kbar_ref[i] for i in range(n_chunks)], axis=1)
            sc = jnp.dot(qbf_ref[...], kbar, precision=lax.Precision.HIGHEST,
                         preferred_element_type=F32)
            blk_idx = lax.broadcasted_iota(jnp.int32, (ncol, nblk), 1)
            picked = _top_k_mask(sc, blk_idx.astype(F32), min(MOBA_TOPK, nblk), 1, nblk)
            bias = jnp.where(picked, 0.0, MASKED)
            s_own = lax.dot_general(qbd, kn_ref[...].astype(BF16), _NT, preferred_element_type=F32)
            qry = lax.broadcasted_iota(jnp.int32, (ncol, lq), 0) % lq
            key = lax.broadcasted_iota(jnp.int32, (ncol, lq), 1)
            s_own = jnp.where(key <= qry, s_own, MASKED)

            def lane_fold(x, op):
                out = x[:, :LANES]
                for i in range(1, MOBA_BLOCK // LANES):
                    out = op(out, x[:, i * LANES:(i + 1) * LANES])
                return out

            m_acc = jnp.full((ncol, LANES), MASKED, F32)
            for j in range(nblk):
                sj = s_ref[j // bps, :, cols(j % bps)] + bias[:, j:j + 1]
                m_acc = jnp.maximum(m_acc, lane_fold(sj, jnp.maximum))
            m = jnp.maximum(jnp.max(s_own, axis=1, keepdims=True), jnp.max(m_acc, axis=1, keepdims=True))
            p_own = jnp.exp2(s_own - m)
            pown_ref[...] = p_own
            l_acc = jnp.zeros((ncol, LANES), F32)
            for j in range(nblk):
                p = jnp.exp2(s_ref[j // bps, :, cols(j % bps)] + (bias[:, j:j + 1] - m))
                l_acc = l_acc + lane_fold(p, jnp.add)
                p_ref[j // bps, :, cols(j % bps)] = p.astype(BF16)
            l_ref[...] = jnp.sum(p_own, axis=1, keepdims=True) + jnp.sum(l_acc, axis=1, keepdims=True)


def _pages_t(cache):
    n_phys, _, page, n_heads, head_dim = cache.shape
    return jnp.transpose(cache[:, 0], (0, 2, 3, 1)).reshape(n_phys, n_heads * head_dim, page)


def _moba_sample(q, k_new, v_new, cache_kt, cache_vt, page_table, n_heads, head_dim):
    n_seq, lq, width = q.shape
    page = cache_kt.shape[2]
    n_pages = page_table.shape[1]
    pps = min(SAMPLE_PAGES_PER_STEP, n_pages)
    n_chunks = n_pages // pps
    bps = pps * page // MOBA_BLOCK
    ncol = n_heads * lq
    last = n_seq - 1

    def seq_spec(idx):
        return pl.BlockSpec((None, lq, width), lambda b, c, pt: (idx(b), 0, 0))

    def page_spec(idx, i):
        return pl.BlockSpec((None, width, page),
                            lambda b, c, pt: (pt[idx(b) * n_pages + c * pps + i], 0, 0))

    cur = lambda b: jnp.minimum(b, last)
    prev = lambda b: jnp.maximum(b - 1, 0)
    grid_spec = pltpu.PrefetchScalarGridSpec(
        num_scalar_prefetch=1,
        grid=(n_seq + 1, n_chunks),
        in_specs=[seq_spec(cur), seq_spec(cur), seq_spec(prev)]
        + [page_spec(cur, i) for i in range(pps)] + [page_spec(prev, i) for i in range(pps)],
        out_specs=seq_spec(prev),
        scratch_shapes=[pltpu.VMEM((n_chunks, ncol, pps * page), F32),
                        pltpu.VMEM((n_chunks, ncol, pps * page), BF16),
                        pltpu.VMEM((n_chunks, width, bps), F32),
                        pltpu.VMEM((ncol, width), F32),
                        pltpu.VMEM((ncol, width), BF16),
                        pltpu.VMEM((ncol, width), F32),
                        pltpu.VMEM((ncol, 1), F32),
                        pltpu.VMEM((ncol, lq), F32)])
    return pl.pallas_call(
        functools.partial(_moba_sample_kernel, pps=pps, n_heads=n_heads, head_dim=head_dim),
        grid_spec=grid_spec,
        out_shape=jax.ShapeDtypeStruct((n_seq, lq, width), BF16),
        compiler_params=_params("arbitrary", "arbitrary"),
        name="moba_sample",
    )(page_table.reshape(-1), q, k_new, v_new, *([cache_kt] * pps), *([cache_vt] * pps))


def _mix_out_kernel(*refs, prompt, start_pos, hist):
    if prompt:
        x_ref, a_ref, u_ref, uh_ref, wpool_ref, ps_ref, wout_ref, g_ref, x2_ref, h_ref, ext_ref = refs
        tm = u_ref.shape[0]
        t = pl.program_id(0)
        ext_ref[0:hist + 1] = jnp.where(t > 0, uh_ref[...], 0.0)
        ext_ref[hist + 1:] = u_ref[...]
        pos = t * tm + lax.broadcasted_iota(jnp.int32, (tm, 1), 0)

        def shifted(i, lanes):
            return ext_ref[pl.ds(hist + 1 - i, tm), lanes]
    else:
        (x_ref, a_ref, u_ref, uh_ref, wpool_ref, ps_ref, wout_ref, g_ref,
         x2_ref, h_ref, state_ref, ext_ref) = refs
        tb, lq = u_ref.shape[0], u_ref.shape[1]
        tm = tb * lq
        ext_ref[:, 1:hist + 1, :] = uh_ref[...]
        ext_ref[:, hist + 1:, :] = u_ref[...]
        state_ref[...] = ext_ref[:, lq + 1:, :]
        pos = start_pos + lax.broadcasted_iota(jnp.int32, (tm, 1), 0) % lq

        def shifted(i, lanes):
            return ext_ref[:, pl.ds(hist + 1 - i, lq), lanes].reshape(tm, lanes.stop - lanes.start)

    gw = wpool_ref.shape[1]
    ys = []
    for g, w in enumerate(POOL_WINDOWS):
        lanes = slice(g * gw, (g + 1) * gw)
        cur = shifted(0, lanes)
        total = cur
        for i in range(1, w):
            total = total + shifted(i, lanes)
        cnt = jnp.minimum(w, pos + 1).astype(F32)
        d = total / cnt - cur
        ys.append(jnp.dot(d.astype(BF16), wpool_ref[g], preferred_element_type=F32))
    pooled = jnp.concatenate(ys, axis=1) * ps_ref[...]
    mixed = jnp.concatenate([a_ref[...], pooled.astype(BF16)], axis=1)
    x2 = x_ref[...] + jnp.dot(mixed, wout_ref[...], preferred_element_type=F32)
    x2_ref[...] = x2
    h_ref[...] = _rmsnorm(x2, g_ref[...]).astype(BF16)


def _mix_out(x, a, u, u_hist, wpool_bf, pool_scale, wout_bf, g_ffn, prompt, start_pos):
    rows, d = x.shape
    pool_w = a.shape[1]
    hist = max(POOL_WINDOWS) - 1
    full = lambda arr: pl.BlockSpec(arr.shape, lambda i: (0,) * arr.ndim)
    row_blk = lambda w: pl.BlockSpec((tm, w), lambda i: (i, 0))
    out_shape = [jax.ShapeDtypeStruct((rows, d), F32), jax.ShapeDtypeStruct((rows, d), BF16)]
    if prompt:
        tm = min(MIX_ROWS, rows)
        halo = hist + 1
        u_specs = [row_blk(pool_w),
                   pl.BlockSpec((halo, pool_w), lambda i: (jnp.maximum(i * (tm // halo) - 1, 0), 0))]
        u_args = (u, u)
        out_specs = [row_blk(d), row_blk(d)]
        scratch = [pltpu.VMEM((tm + halo, pool_w), F32)]
        grid = (rows // tm,)
    else:
        n_seq, lq = u.shape[0], u.shape[1]
        tm = rows
        u_specs = [pl.BlockSpec((n_seq, lq, pool_w), lambda i: (0, 0, 0)),
                   pl.BlockSpec((n_seq, hist, pool_w), lambda i: (0, 0, 0))]
        u_args = (u, u_hist)
        out_shape.append(jax.ShapeDtypeStruct((n_seq, hist, pool_w), F32))
        out_specs = [row_blk(d), row_blk(d), pl.BlockSpec((n_seq, hist, pool_w), lambda i: (0, 0, 0))]
        scratch = [pltpu.VMEM((n_seq, hist + 1 + lq, pool_w), F32)]
        grid = (1,)
    return pl.pallas_call(
        functools.partial(_mix_out_kernel, prompt=prompt, start_pos=start_pos, hist=hist),
        grid=grid,
        in_specs=[row_blk(d), row_blk(a.shape[1])] + u_specs
        + [full(wpool_bf), full(pool_scale), full(wout_bf), full(g_ffn)],
        out_specs=out_specs, out_shape=out_shape, scratch_shapes=scratch,
        compiler_params=_params("parallel"),
        name="mix_out_prompt" if prompt else "mix_out_sample",
    )(x, a, *u_args, wpool_bf, pool_scale, wout_bf, g_ffn)


def _ffn_tail_kernel(*refs, prompt):
    if prompt:
        (h_ref, hh_ref, x2_ref, p_ref, wg_ref, wv_ref, cwg_ref, cwv_ref, cbg_ref, cbv_ref, wd_ref,
         wpg_ref, wpp_ref, gf_ref, y_ref, sg_ref, sv_ref, hext_ref, eg_ref, ev_ref, acc_ref) = refs
    else:
        (h_ref, hg_ref, hv_ref, x2_ref, p_ref, wg_ref, wv_ref, cwg_ref, cwv_ref, cbg_ref, cbv_ref,
         wd_ref, wpg_ref, wpp_ref, gf_ref, y_ref, sg_ref, sv_ref, eg_ref, ev_ref, acc_ref) = refs
    t, c = pl.program_id(0), pl.program_id(1)
    tm = h_ref.shape[0]
    keep = CONV_W - 1
    n_split = FFN_ROW_SPLITS if prompt else 1
    rs = tm // n_split

    @pl.when(c == 0)
    def _():
        acc_ref[...] = jnp.zeros(acc_ref.shape, F32)

    if prompt:
        halo = hh_ref.shape[0]

        @pl.when(c == 0)
        def _():
            hext_ref[0:halo] = jnp.where(t > 0, hh_ref[...], jnp.zeros(hh_ref.shape, hh_ref.dtype))
            hext_ref[halo:] = h_ref[...]

        for q in range(n_split):
            rows = slice(0 if q == 0 else halo + q * rs, halo + (q + 1) * rs)
            hx = hext_ref[rows]
            eg_ref[rows] = jnp.dot(hx, wg_ref[...], preferred_element_type=F32)
            ev_ref[rows] = jnp.dot(hx, wv_ref[...], preferred_element_type=F32)
        sg_ref[...] = eg_ref[tm + halo - 8:, :]
        sv_ref[...] = ev_ref[tm + halo - 8:, :]

        def delayed(ref, i, q):
            return ref[pl.ds(halo + q * rs - i, rs), :]
    else:
        tb, lq = eg_ref.shape[0], eg_ref.shape[1] - 8
        fc = eg_ref.shape[2]
        hx = h_ref[...]
        eg_ref[:, 8:, :] = jnp.dot(hx, wg_ref[...], preferred_element_type=F32).reshape(tb, lq, fc)
        ev_ref[:, 8:, :] = jnp.dot(hx, wv_ref[...], preferred_element_type=F32).reshape(tb, lq, fc)
        eg_ref[:, 8 - keep:8, :] = hg_ref[...]
        ev_ref[:, 8 - keep:8, :] = hv_ref[...]
        sg_ref[...] = eg_ref[:, 8 + lq - keep:, :]
        sv_ref[...] = ev_ref[:, 8 + lq - keep:, :]

        def delayed(ref, i, q):
            return ref[:, pl.ds(8 - i, lq), :].reshape(tm, fc)

    def conv(ref, cw_ref, cb_ref, q):
        cw = cw_ref[...]
        out = cb_ref[...]
        for i in range(CONV_W):
            out = out + delayed(ref, CONV_W - 1 - i, q) * cw[i:i + 1, :]
        return out

    for q in range(n_split):
        gate = conv(eg_ref, cwg_ref, cbg_ref, q)
        val = conv(ev_ref, cwv_ref, cbv_ref, q)
        act = (gate * jax.nn.sigmoid(gate) * val).astype(BF16)
        acc_ref[q * rs:(q + 1) * rs] += jnp.dot(act, wd_ref[...], preferred_element_type=F32)

    @pl.when(c == pl.num_programs(1) - 1)
    def _():
        x3 = x2_ref[...] + acc_ref[...]
        gate_p = jax.nn.sigmoid(jnp.dot(x3.astype(BF16), wpg_ref[...], preferred_element_type=F32))
        emb = jnp.dot(p_ref[...].astype(BF16), wpp_ref[...], preferred_element_type=F32)
        y_ref[...] = _rmsnorm(x3 + gate_p * emb, gf_ref[...])


def _ffn_tail(h, x2, p, state, wup_bf, conv_w, conv_b, wdown_bf, wpg_bf, wpp_bf, g_final, prompt):
    rows, d = x2.shape
    d_ff = wdown_bf.shape[0]
    fc = FFN_CHUNK
    n_fc = d_ff // fc
    p_dim = p.shape[1]
    tm = min(FFN_ROWS, rows)
    n_t = rows // tm
    row_blk = lambda w: pl.BlockSpec((tm, w), lambda t, c: (t, 0))
    full = lambda arr: pl.BlockSpec(arr.shape, lambda t, c: (0,) * arr.ndim)
    gate_cols = lambda r: pl.BlockSpec((r, fc), lambda t, c: (0, c))
    val_cols = lambda r: pl.BlockSpec((r, fc), lambda t, c: (0, c + n_fc))
    weights = [gate_cols(d), val_cols(d), gate_cols(CONV_W), val_cols(CONV_W), gate_cols(1), val_cols(1),
               pl.BlockSpec((fc, d), lambda t, c: (c, 0)), full(wpg_bf), full(wpp_bf), full(g_final)]
    weight_args = (wup_bf, wup_bf, conv_w, conv_w, conv_b, conv_b, wdown_bf, wpg_bf, wpp_bf, g_final)
    y_shape = jax.ShapeDtypeStruct((rows, d), F32)
    if prompt:
        halo = BF16_SUBLANES
        acts = [row_blk(d),
                pl.BlockSpec((halo, d), lambda t, c: (jnp.maximum(t * (tm // halo) - 1, 0), 0)),
                row_blk(d), row_blk(p_dim)]
        act_args = (h, h, x2, p)
        st_shape = jax.ShapeDtypeStruct((n_t * 8, d_ff), F32)
        st_spec = pl.BlockSpec((8, fc), lambda t, c: (t, c))
        scratch = [pltpu.VMEM((tm + halo, d), BF16), pltpu.VMEM((tm + halo, fc), F32),
                   pltpu.VMEM((tm + halo, fc), F32), pltpu.VMEM((tm, d), F32)]
    else:
        n_seq, keep = state.shape[0], state.shape[1]
        lq = rows // n_seq
        hist_g = pl.BlockSpec((n_seq, keep, fc), lambda t, c: (0, 0, c))
        hist_v = pl.BlockSpec((n_seq, keep, fc), lambda t, c: (0, 0, c + n_fc))
        acts = [row_blk(d), hist_g, hist_v, row_blk(d), row_blk(p_dim)]
        act_args = (h, state, state, x2, p)
        st_shape = jax.ShapeDtypeStruct((n_seq, keep, d_ff), F32)
        st_spec = pl.BlockSpec((n_seq, keep, fc), lambda t, c: (0, 0, c))
        scratch = [pltpu.VMEM((n_seq, 8 + lq, fc), F32), pltpu.VMEM((n_seq, 8 + lq, fc), F32),
                   pltpu.VMEM((tm, d), F32)]
    return pl.pallas_call(
        functools.partial(_ffn_tail_kernel, prompt=prompt),
        grid=(n_t, n_fc),
        in_specs=acts + weights,
        out_specs=[row_blk(d), st_spec, st_spec],
        out_shape=[y_shape, st_shape, st_shape],
        scratch_shapes=scratch,
        compiler_params=_params("parallel", "arbitrary"),
        name="ffn_tail_prompt" if prompt else "ffn_tail_sample",
    )(*act_args, *weight_args)


def kernel(x_prompt, x_sample, cache_k, cache_v, state_pool, state_conv, page_table, p_prompt, p_sample,
           g_mix, w_in, w_out, w_pool, pool_scale, g_ffn, w_up, conv_w, conv_b, w_down, w_pg, w_pp, g_final):
    b_p, s_len, d = x_prompt.shape
    n_seq, lq, _ = x_sample.shape
    n_phys, depth, page, n_heads, head_dim = cache_k.shape
    attn_w = n_heads * head_dim
    pool_w = d - attn_w
    past_len = page_table.shape[1] * page
    hist = state_pool.shape[2]
    keep = state_conv.shape[2]
    d_ff = w_down.shape[1]
    assert b_p == 1 and depth == 1
    assert past_len % MOBA_BLOCK == 0 and MOBA_BLOCK % page == 0 and s_len % MOBA_BLOCK == 0
    assert lq == 8 and hist == max(POOL_WINDOWS) - 1 and keep == CONV_W - 1
    assert d_ff % FFN_CHUNK == 0

    i = 0
    row = lambda v: v.reshape(1, -1)
    w_in_bf, w_out_bf, w_pool_bf = w_in[i].astype(BF16), w_out[i].astype(BF16), w_pool[i].astype(BF16)
    w_up_bf, w_down_bf = w_up[i].astype(BF16), w_down[i].astype(BF16)
    w_pg_bf, w_pp_bf = w_pg[i].astype(BF16), w_pp[i].astype(BF16)

    xp = x_prompt.reshape(s_len, d)
    qt, k_p, v_p, u_p, kb, vt, kbar = _in_proj(xp, row(g_mix[i]), w_in_bf, jnp.arange(s_len),
                                               attn_w, head_dim, True)
    a_p = _moba_prompt(qt, kb, vt, kbar.reshape(-1, attn_w), head_dim)
    x2_p, h_p = _mix_out(xp, a_p, u_p, None, w_pool_bf, row(pool_scale[i]), w_out_bf, row(g_ffn[i]), True, 0)
    y_p, cg_p, cv_p = _ffn_tail(h_p, x2_p, p_prompt[i].reshape(s_len, -1), None, w_up_bf, conv_w[i],
                                row(conv_b[i]), w_down_bf, w_pg_bf, w_pp_bf, row(g_final), True)

    rows_s = n_seq * lq
    xs = x_sample.reshape(rows_s, d)
    pos_s = past_len + jnp.arange(rows_s) % lq
    q_s, k_s, v_s, u_s = _in_proj(xs, row(g_mix[i]), w_in_bf, pos_s, attn_w, head_dim, False)
    seq3 = lambda v: v.reshape(n_seq, lq, -1)
    a_s = _moba_sample(seq3(q_s), seq3(k_s), seq3(v_s), _pages_t(cache_k), _pages_t(cache_v), page_table,
                       n_heads, head_dim)
    x2_s, h_s, pool_s = _mix_out(xs, a_s.reshape(rows_s, attn_w), seq3(u_s), state_pool[:, i], w_pool_bf,
                                 row(pool_scale[i]), w_out_bf, row(g_ffn[i]), False, past_len)
    y_s, cg_s, cv_s = _ffn_tail(h_s, x2_s, p_sample[i].reshape(rows_s, -1), state_conv[:, i], w_up_bf,
                                conv_w[i], row(conv_b[i]), w_down_bf, w_pg_bf, w_pp_bf, row(g_final), False)

    heads5 = lambda v, n, l: v.reshape(n, 1, l, n_heads, head_dim)
    conv_p = jnp.concatenate([cg_p[-keep:], cv_p[-keep:]], axis=1).reshape(1, 1, keep, 2 * d_ff)
    conv_s = jnp.concatenate([cg_s, cv_s], axis=2).reshape(n_seq, 1, keep, 2 * d_ff)
    return (y_p.reshape(1, s_len, d), y_s.reshape(n_seq, lq, d),
            heads5(k_p, 1, s_len), heads5(v_p, 1, s_len), heads5(k_s, n_seq, lq), heads5(v_s, n_seq, lq),
            u_p[-hist:].reshape(1, 1, hist, pool_w), pool_s.reshape(n_seq, 1, hist, pool_w),
            conv_p, conv_s)
```

```python
import functools
import math

import jax
import jax.numpy as jnp
from jax import lax
from jax.experimental import pallas as pl
from jax.experimental.pallas import tpu as pltpu

F32 = jnp.float32
BF16 = jnp.bfloat16

MOBA_BLOCK = 256
MOBA_TOPK = 3
ROPE_THETA = 10000.0
POOL_WINDOWS = (2, 4, 8, 16)
CONV_W = 3
EPS = 1e-6
MASKED = -1e30
LOG2_E = math.log2(math.e)

LANES = 128
BF16_SUBLANES = 16
VMEM_LIMIT = 56 * 1024 * 1024

IN_PROJ_ROWS = 512
MIX_ROWS = 512
FFN_ROWS = 1024
FFN_CHUNK = 256
FFN_ROW_SPLITS = 4
PROMPT_SCORE_BUFFERS = 4
PROMPT_UNROLL = 8
SAMPLE_PAGES_PER_STEP = 32

_NT = (((1,), (1,)), ((), ()))


def _params(*sem):
    return pltpu.CompilerParams(dimension_semantics=sem, vmem_limit_bytes=VMEM_LIMIT)


def _rmsnorm(x, g):
    return x * lax.rsqrt(jnp.mean(x * x, axis=-1, keepdims=True) + EPS) * g


def _top_k_mask(s, idx, k, axis, n):
    picked = jnp.zeros(s.shape, jnp.bool_)
    for _ in range(k):
        mx = jnp.max(s, axis=axis, keepdims=True)
        first = jnp.min(jnp.where(s == mx, idx, float(n)), axis=axis, keepdims=True)
        hit = idx == first
        picked = picked | hit
        s = jnp.where(hit, -jnp.inf, s)
    return picked


def _in_proj_kernel(x_ref, g_ref, w_ref, cos_ref, sin_ref, *outs, attn_w, head_dim, prompt):
    tm = x_ref.shape[0]
    xn = _rmsnorm(x_ref[...], g_ref[...])
    z = jnp.dot(xn.astype(BF16), w_ref[...], preferred_element_type=F32)
    reps = attn_w // LANES
    cos = jnp.tile(cos_ref[...], (1, reps))
    sin = jnp.tile(sin_ref[...], (1, reps))
    half = head_dim // 2
    lane = lax.broadcasted_iota(jnp.int32, (tm, attn_w), 1)
    lower = (lane % head_dim) < half

    def rope(t):
        rot = jnp.where(lower, pltpu.roll(t, attn_w - half, 1), pltpu.roll(t, half, 1))
        return t * cos + rot * sin

    q = rope(z[:, :attn_w]) * (head_dim ** -0.5 * LOG2_E)
    k = rope(z[:, attn_w:2 * attn_w])
    v = z[:, 2 * attn_w:3 * attn_w]
    u = z[:, 3 * attn_w:]
    if prompt:
        qt_ref, k_ref, v_ref, u_ref, kb_ref, vt_ref, kbar_ref = outs
        qt_ref[...] = q.T
        kb_ref[...] = k.astype(BF16)
        vt = v.T.astype(BF16)
        ones = jnp.ones((BF16_SUBLANES, tm), BF16)
        vt = jnp.concatenate([piece for h in range(attn_w // head_dim)
                              for piece in (vt[h * head_dim:(h + 1) * head_dim], ones)], axis=0)
        for i in range(tm // MOBA_BLOCK):
            rows = slice(i * MOBA_BLOCK, (i + 1) * MOBA_BLOCK)
            vt_ref[i] = vt[:, rows]
            kbar_ref[i] = jnp.sum(k[rows], axis=0, keepdims=True) * (1.0 / MOBA_BLOCK)
    else:
        q_ref, k_ref, v_ref, u_ref = outs
        q_ref[...] = q
    k_ref[...] = k
    v_ref[...] = v
    u_ref[...] = u


def _rope_tables(pos, head_dim):
    inv = ROPE_THETA ** (-jnp.arange(0, head_dim, 2, dtype=F32) / head_dim)
    ang = pos.astype(F32)[:, None] * inv[None, :]
    c, s = jnp.cos(ang), jnp.sin(ang)
    reps = LANES // head_dim
    return jnp.tile(jnp.concatenate([c, c], 1), (1, reps)), jnp.tile(jnp.concatenate([-s, s], 1), (1, reps))


def _in_proj(x, g, w_bf, pos, attn_w, head_dim, prompt):
    rows, d = x.shape
    tm = min(IN_PROJ_ROWS, rows)
    cos, sin = _rope_tables(pos, head_dim)
    row_blk = lambda w: pl.BlockSpec((tm, w), lambda i: (i, 0))
    full = lambda a: pl.BlockSpec(a.shape, lambda i: (0,) * a.ndim)
    nat = jax.ShapeDtypeStruct((rows, attn_w), F32)
    if prompt:
        nblk, bpt = rows // MOBA_BLOCK, tm // MOBA_BLOCK
        vt_rows = attn_w // head_dim * (head_dim + BF16_SUBLANES)
        out_shape = [jax.ShapeDtypeStruct((attn_w, rows), F32), nat, nat, nat,
                     jax.ShapeDtypeStruct((rows, attn_w), BF16),
                     jax.ShapeDtypeStruct((nblk, vt_rows, MOBA_BLOCK), BF16),
                     jax.ShapeDtypeStruct((nblk, 1, attn_w), F32)]
        out_specs = [pl.BlockSpec((attn_w, tm), lambda i: (0, i)), row_blk(attn_w), row_blk(attn_w),
                     row_blk(attn_w), row_blk(attn_w),
                     pl.BlockSpec((bpt, vt_rows, MOBA_BLOCK), lambda i: (i, 0, 0)),
                     pl.BlockSpec((bpt, 1, attn_w), lambda i: (i, 0, 0))]
    else:
        out_shape = [nat, nat, nat, nat]
        out_specs = [row_blk(attn_w)] * 4
    return pl.pallas_call(
        functools.partial(_in_proj_kernel, attn_w=attn_w, head_dim=head_dim, prompt=prompt),
        grid=(rows // tm,),
        in_specs=[row_blk(d), full(g), full(w_bf), row_blk(LANES), row_blk(LANES)],
        out_specs=out_specs, out_shape=out_shape,
        compiler_params=_params("parallel"),
        name="in_proj_prompt" if prompt else "in_proj_sample",
    )(x, g, w_bf, cos, sin)


def _moba_prompt_kernel(qt_ref, kb_ref, vt_ref, kbar_ref, o_ref,
                        qm_ref, bias_ref, m_ref, acc_ref, s_all_ref, c_all_ref, *, head_dim, heads):
    t = pl.program_id(1)
    v_rows = head_dim + BF16_SUBLANES
    nblk = kbar_ref.shape[0]
    tq = qt_ref.shape[1]
    qt = qt_ref[...]
    row_head = lax.broadcasted_iota(jnp.int32, qt.shape, 0) // head_dim
    blk_idx = lax.broadcasted_iota(jnp.int32, (nblk, tq), 0)
    past = blk_idx < t
    key_pos = lax.broadcasted_iota(jnp.int32, (MOBA_BLOCK, tq), 0)
    qry_pos = lax.broadcasted_iota(jnp.int32, (MOBA_BLOCK, tq), 1)
    causal = key_pos <= qry_pos
    kbar = kbar_ref[...]
    k_own = kb_ref[t]
    topk = min(MOBA_TOPK, nblk)

    def v_t(j, h):
        return vt_ref[j, h * v_rows:(h + 1) * v_rows, :]

    bufs = [(s_all_ref.at[i], c_all_ref.at[i]) for i in range(s_all_ref.shape[0])]

    def score_head(j, buf, h):
        s_ref, c_ref = buf
        s = jnp.dot(kb_ref[j], qm_ref[h], preferred_element_type=F32)
        s_ref[h] = s
        c_ref[h] = jnp.max(s, axis=0, keepdims=True)

    def absorb_head(j, buf, h):
        s_ref, c_ref = buf
        bias = bias_ref[h, pl.ds(j, 1), :]
        m_prev = m_ref[h]
        m_new = jnp.maximum(m_prev, c_ref[h] + bias)
        alpha = jnp.exp2(m_prev - m_new)
        p = jnp.exp2(s_ref[h] - (m_new - bias))
        acc_ref[h] = alpha * acc_ref[h] + jnp.dot(v_t(j, h), p.astype(BF16),
                                                  preferred_element_type=F32)
        m_ref[h] = m_new

    def scores(j, buf):
        for h in range(heads):
            score_head(j, buf, h)

    def absorb(j, buf):
        for h in range(heads):
            absorb_head(j, buf, h)

    for h in range(heads):
        qm_ref[h] = jnp.where(row_head == h, qt, 0.0).astype(BF16)
    scores(0, bufs[0])
    for h in range(heads):
        qm = jnp.where(row_head == h, qt, 0.0)
        sc = jnp.dot(kbar, qm, precision=lax.Precision.HIGHEST, preferred_element_type=F32)
        picked = _top_k_mask(jnp.where(past, sc, -jnp.inf), blk_idx.astype(F32), topk, 0, nblk)
        bias_ref[h] = jnp.where(picked & past, 0.0, MASKED)
        s = jnp.dot(k_own, qm_ref[h], preferred_element_type=F32)
        s = jnp.where(causal, s, MASKED)
        m0 = jnp.max(s, axis=0, keepdims=True)
        p = jnp.exp2(s - m0)
        m_ref[h] = m0
        acc_ref[h] = jnp.dot(v_t(t, h), p.astype(BF16), preferred_element_type=F32)

    scores(jnp.minimum(1, nblk - 1), bufs[1])
    ahead = len(bufs) // 2

    def run(j0, n):
        for r in range(n):
            j_ahead = jnp.minimum(j0 + r + ahead, nblk - 1)
            for h in range(heads):
                score_head(j_ahead, bufs[(r + ahead) % len(bufs)], h)
                absorb_head(j0 + r, bufs[r % len(bufs)], h)

    def group(i, carry):
        run(PROMPT_UNROLL * i, PROMPT_UNROLL)
        return carry

    lax.fori_loop(0, t // PROMPT_UNROLL, group, 0)
    unroll = len(bufs)
    if PROMPT_UNROLL > unroll:
        @pl.when(t % PROMPT_UNROLL >= unroll)
        def _():
            run(t // PROMPT_UNROLL * PROMPT_UNROLL, unroll)
    base, rest = t // unroll * unroll, t % unroll
    for r in range(unroll - 1):
        if r >= ahead:
            @pl.when(rest > r)
            def _():
                scores(base + r, bufs[r % len(bufs)])

        @pl.when(rest > r)
        def _():
            absorb(base + r, bufs[r % len(bufs)])

    o = jnp.concatenate([acc_ref[h, :head_dim, :] / acc_ref[h, head_dim:head_dim + 1, :]
                         for h in range(heads)], axis=0)
    o_ref[...] = o.T.astype(o_ref.dtype)


def _moba_prompt(qt, kb, vt, kbar, head_dim):
    attn_w, s_len = qt.shape
    nblk = s_len // MOBA_BLOCK
    gw = 2 * LANES
    heads = gw // head_dim
    kb3 = kb.reshape(nblk, MOBA_BLOCK, attn_w)
    v_rows = head_dim + BF16_SUBLANES
    return pl.pallas_call(
        functools.partial(_moba_prompt_kernel, head_dim=head_dim, heads=heads),
        grid=(attn_w // gw, nblk),
        in_specs=[pl.BlockSpec((gw, MOBA_BLOCK), lambda g, t: (g, t)),
                  pl.BlockSpec((nblk, MOBA_BLOCK, gw), lambda g, t: (0, 0, g)),
                  pl.BlockSpec((nblk, heads * v_rows, MOBA_BLOCK), lambda g, t: (0, g, 0)),
                  pl.BlockSpec((nblk, gw), lambda g, t: (0, g))],
        out_specs=pl.BlockSpec((MOBA_BLOCK, gw), lambda g, t: (t, g)),
        out_shape=jax.ShapeDtypeStruct((s_len, attn_w), BF16),
        scratch_shapes=[pltpu.VMEM((heads, gw, MOBA_BLOCK), BF16),
                        pltpu.VMEM((heads, nblk, MOBA_BLOCK), F32),
                        pltpu.VMEM((heads, 1, MOBA_BLOCK), F32),
                        pltpu.VMEM((heads, v_rows, MOBA_BLOCK), F32),
                        pltpu.VMEM((PROMPT_SCORE_BUFFERS, heads, MOBA_BLOCK, MOBA_BLOCK), F32),
                        pltpu.VMEM((PROMPT_SCORE_BUFFERS, heads, 1, MOBA_BLOCK), F32)],
        compiler_params=_params("parallel", "arbitrary"),
        name="moba_prompt",
    )(qt, kb3, vt, kbar)


def _moba_sample_kernel(pt_ref, q_ref, kn_ref, vn_ref, *rest, pps, n_heads, head_dim):
    del pt_ref
    k_refs, v_refs = rest[:pps], rest[pps:2 * pps]
    o_ref, s_ref, p_ref, kbar_ref, acc_ref, qbd_ref, qbf_ref, l_ref, pown_ref, btile_ref = rest[2 * pps:]
    b, c = pl.program_id(0), pl.program_id(1)
    n_seq, n_chunks = pl.num_programs(0) - 1, s_ref.shape[0]
    lq, width = q_ref.shape
    page = k_refs[0].shape[1]
    ncol = n_heads * lq
    ppb = MOBA_BLOCK // page
    bps = pps // ppb
    nblk = n_chunks * bps

    def block_t(refs, jj):
        return jnp.concatenate([refs[jj * ppb + i][...] for i in range(ppb)], axis=1)

    def cols(jj):
        return slice(jj * MOBA_BLOCK, (jj + 1) * MOBA_BLOCK)

    @pl.when(b > 0)
    def _apply_values():
        acc = jnp.zeros((ncol, width), F32)
        for jj in range(bps):
            acc += lax.dot_general(p_ref[c, :, cols(jj)], block_t(v_refs, jj).astype(BF16), _NT,
                                   preferred_element_type=F32)

        @pl.when(c == 0)
        def _():
            acc_ref[...] = acc

        @pl.when(c > 0)
        def _():
            acc_ref[...] += acc

        @pl.when(c == n_chunks - 1)
        def _finish():
            o = acc_ref[...]
            vn, pown = vn_ref[...], pown_ref[...]
            for j in range(lq):
                o = o + pown[:, j:j + 1] * vn[j:j + 1, :]
            o = o / l_ref[...]
            lane_head = lax.broadcasted_iota(jnp.int32, (lq, width), 1) // head_dim
            out = jnp.zeros((lq, width), F32)
            for h in range(n_heads):
                out = jnp.where(lane_head == h, o[h * lq:(h + 1) * lq, :], out)
            o_ref[...] = out.astype(o_ref.dtype)

    @pl.when(b < n_seq)
    def _score_keys():
        @pl.when(c == 0)
        def _():
            q_rep = jnp.concatenate([q_ref[...]] * n_heads, axis=0)
            row_head = lax.broadcasted_iota(jnp.int32, (ncol, width), 0) // lq
            lane_head = lax.broadcasted_iota(jnp.int32, (ncol, width), 1) // head_dim
            qbd = jnp.where(row_head == lane_head, q_rep, 0.0)
            qbf_ref[...] = qbd
            qbd_ref[...] = qbd.astype(BF16)

        qbd = qbd_ref[...]
        for jj in range(bps):
            kt = block_t(k_refs, jj)
            kbar_ref[c, :, jj:jj + 1] = jnp.sum(kt, axis=1, keepdims=True) * (1.0 / MOBA_BLOCK)
            s_ref[c, :, cols(jj)] = jnp.dot(qbd, kt.astype(BF16), preferred_element_type=F32)

        @pl.when(c == n_chunks - 1)
        def _select_and_normalise():
            kbar = jnp.concatenate([kbar_ref[i] for i in range(n_chunks)], axis=1)
            sc = jnp.dot(qbf_ref[...], kbar, precision=lax.Precision.HIGHEST,
                         preferred_element_type=F32)
            blk_idx = lax.broadcasted_iota(jnp.int32, (ncol, nblk), 1)
            picked = _top_k_mask(sc, blk_idx.astype(F32), min(MOBA_TOPK, nblk), 1, nblk)
            bias = jnp.where(picked, 0.0, MASKED)
            s_own = lax.dot_general(qbd, kn_ref[...].astype(BF16), _NT, preferred_element_type=F32)
            qry = lax.broadcasted_iota(jnp.int32, (ncol, lq), 0) % lq
            key = lax.broadcasted_iota(jnp.int32, (ncol, lq), 1)
            s_own = jnp.where(key <= qry, s_own, MASKED)

            def lane_fold(x, op):
                out = x[:, :LANES]
                for i in range(1, MOBA_BLOCK // LANES):
                    out = op(out, x[:, i * LANES:(i + 1) * LANES])
                return out

            m_acc = jnp.full((ncol, LANES), MASKED, F32)
            for j in range(nblk):
                bias_j = jnp.broadcast_to(bias[:, j:j + 1], (ncol, LANES))
                btile_ref[j] = bias_j
                m_acc = jnp.maximum(m_acc, lane_fold(s_ref[j // bps, :, cols(j % bps)], jnp.maximum) + bias_j)
            m = jnp.maximum(jnp.max(s_own, axis=1, keepdims=True), jnp.max(m_acc, axis=1, keepdims=True))
            p_own = jnp.exp2(s_own - m)
            pown_ref[...] = p_own
            m_tile = jnp.broadcast_to(m, (ncol, LANES))
            l_acc = jnp.zeros((ncol, LANES), F32)
            for j in range(nblk):
                shift = jnp.tile(btile_ref[j] - m_tile, (1, MOBA_BLOCK // LANES))
                p = jnp.exp2(s_ref[j // bps, :, cols(j % bps)] + shift)
                l_acc = l_acc + lane_fold(p, jnp.add)
                p_ref[j // bps, :, cols(j % bps)] = p.astype(BF16)
            l_ref[...] = jnp.sum(p_own, axis=1, keepdims=True) + jnp.sum(l_acc, axis=1, keepdims=True)


def _pages_t(cache):
    n_phys, _, page, n_heads, head_dim = cache.shape
    return jnp.transpose(cache[:, 0], (0, 2, 3, 1)).reshape(n_phys, n_heads * head_dim, page)


def _moba_sample(q, k_new, v_new, cache_kt, cache_vt, page_table, n_heads, head_dim):
    n_seq, lq, width = q.shape
    page = cache_kt.shape[2]
    n_pages = page_table.shape[1]
    pps = min(SAMPLE_PAGES_PER_STEP, n_pages)
    n_chunks = n_pages // pps
    bps = pps * page // MOBA_BLOCK
    ncol = n_heads * lq
    last = n_seq - 1

    def seq_spec(idx):
        return pl.BlockSpec((None, lq, width), lambda b, c, pt: (idx(b), 0, 0))

    def page_spec(idx, i):
        return pl.BlockSpec((None, width, page),
                            lambda b, c, pt: (pt[idx(b) * n_pages + c * pps + i], 0, 0))

    cur = lambda b: jnp.minimum(b, last)
    prev = lambda b: jnp.maximum(b - 1, 0)
    grid_spec = pltpu.PrefetchScalarGridSpec(
        num_scalar_prefetch=1,
        grid=(n_seq + 1, n_chunks),
        in_specs=[seq_spec(cur), seq_spec(cur), seq_spec(prev)]
        + [page_spec(cur, i) for i in range(pps)] + [page_spec(prev, i) for i in range(pps)],
        out_specs=seq_spec(prev),
        scratch_shapes=[pltpu.VMEM((n_chunks, ncol, pps * page), F32),
                        pltpu.VMEM((n_chunks, ncol, pps * page), BF16),
                        pltpu.VMEM((n_chunks, width, bps), F32),
                        pltpu.VMEM((ncol, width), F32),
                        pltpu.VMEM((ncol, width), BF16),
                        pltpu.VMEM((ncol, width), F32),
                        pltpu.VMEM((ncol, 1), F32),
                        pltpu.VMEM((ncol, lq), F32),
                        pltpu.VMEM((n_chunks * bps, ncol, LANES), F32)])
    return pl.pallas_call(
        functools.partial(_moba_sample_kernel, pps=pps, n_heads=n_heads, head_dim=head_dim),
        grid_spec=grid_spec,
        out_shape=jax.ShapeDtypeStruct((n_seq, lq, width), BF16),
        compiler_params=_params("arbitrary", "arbitrary"),
        name="moba_sample",
    )(page_table.reshape(-1), q, k_new, v_new, *([cache_kt] * pps), *([cache_vt] * pps))


def _mix_out_kernel(*refs, prompt, start_pos, hist):
    if prompt:
        x_ref, a_ref, u_ref, uh_ref, wpool_ref, ps_ref, wout_ref, g_ref, x2_ref, h_ref, ext_ref = refs
        tm = u_ref.shape[0]
        t = pl.program_id(0)
        ext_ref[0:hist + 1] = jnp.where(t > 0, uh_ref[...], 0.0)
        ext_ref[hist + 1:] = u_ref[...]
        pos = t * tm + lax.broadcasted_iota(jnp.int32, (tm, 1), 0)

        def shifted(i, lanes):
            return ext_ref[pl.ds(hist + 1 - i, tm), lanes]
    else:
        (x_ref, a_ref, u_ref, uh_ref, wpool_ref, ps_ref, wout_ref, g_ref,
         x2_ref, h_ref, state_ref, ext_ref) = refs
        tb, lq = u_ref.shape[0], u_ref.shape[1]
        tm = tb * lq
        ext_ref[:, 1:hist + 1, :] = uh_ref[...]
        ext_ref[:, hist + 1:, :] = u_ref[...]
        state_ref[...] = ext_ref[:, lq + 1:, :]
        pos = start_pos + lax.broadcasted_iota(jnp.int32, (tm, 1), 0) % lq

        def shifted(i, lanes):
            return ext_ref[:, pl.ds(hist + 1 - i, lq), lanes].reshape(tm, lanes.stop - lanes.start)

    gw = wpool_ref.shape[1]
    ys = []
    for g, w in enumerate(POOL_WINDOWS):
        lanes = slice(g * gw, (g + 1) * gw)
        cur = shifted(0, lanes)
        total = cur
        for i in range(1, w):
            total = total + shifted(i, lanes)
        cnt = jnp.minimum(w, pos + 1).astype(F32)
        d = total / cnt - cur
        ys.append(jnp.dot(d.astype(BF16), wpool_ref[g], preferred_element_type=F32))
    pooled = jnp.concatenate(ys, axis=1) * ps_ref[...]
    mixed = jnp.concatenate([a_ref[...], pooled.astype(BF16)], axis=1)
    x2 = x_ref[...] + jnp.dot(mixed, wout_ref[...], preferred_element_type=F32)
    x2_ref[...] = x2
    h_ref[...] = _rmsnorm(x2, g_ref[...]).astype(BF16)


def _mix_out(x, a, u, u_hist, wpool_bf, pool_scale, wout_bf, g_ffn, prompt, start_pos):
    rows, d = x.shape
    pool_w = a.shape[1]
    hist = max(POOL_WINDOWS) - 1
    full = lambda arr: pl.BlockSpec(arr.shape, lambda i: (0,) * arr.ndim)
    row_blk = lambda w: pl.BlockSpec((tm, w), lambda i: (i, 0))
    out_shape = [jax.ShapeDtypeStruct((rows, d), F32), jax.ShapeDtypeStruct((rows, d), BF16)]
    if prompt:
        tm = min(MIX_ROWS, rows)
        halo = hist + 1
        u_specs = [row_blk(pool_w),
                   pl.BlockSpec((halo, pool_w), lambda i: (jnp.maximum(i * (tm // halo) - 1, 0), 0))]
        u_args = (u, u)
        out_specs = [row_blk(d), row_blk(d)]
        scratch = [pltpu.VMEM((tm + halo, pool_w), F32)]
        grid = (rows // tm,)
    else:
        n_seq, lq = u.shape[0], u.shape[1]
        tm = rows
        u_specs = [pl.BlockSpec((n_seq, lq, pool_w), lambda i: (0, 0, 0)),
                   pl.BlockSpec((n_seq, hist, pool_w), lambda i: (0, 0, 0))]
        u_args = (u, u_hist)
        out_shape.append(jax.ShapeDtypeStruct((n_seq, hist, pool_w), F32))
        out_specs = [row_blk(d), row_blk(d), pl.BlockSpec((n_seq, hist, pool_w), lambda i: (0, 0, 0))]
        scratch = [pltpu.VMEM((n_seq, hist + 1 + lq, pool_w), F32)]
        grid = (1,)
    return pl.pallas_call(
        functools.partial(_mix_out_kernel, prompt=prompt, start_pos=start_pos, hist=hist),
        grid=grid,
        in_specs=[row_blk(d), row_blk(a.shape[1])] + u_specs
        + [full(wpool_bf), full(pool_scale), full(wout_bf), full(g_ffn)],
        out_specs=out_specs, out_shape=out_shape, scratch_shapes=scratch,
        compiler_params=_params("parallel"),
        name="mix_out_prompt" if prompt else "mix_out_sample",
    )(x, a, *u_args, wpool_bf, pool_scale, wout_bf, g_ffn)


def _ffn_tail_kernel(*refs, prompt):
    if prompt:
        (h_ref, hh_ref, x2_ref, p_ref, wg_ref, wv_ref, cwg_ref, cwv_ref, cbg_ref, cbv_ref, wd_ref,
         wpg_ref, wpp_ref, gf_ref, y_ref, sg_ref, sv_ref, hext_ref, eg_ref, ev_ref, acc_ref) = refs
    else:
        (h_ref, hg_ref, hv_ref, x2_ref, p_ref, wg_ref, wv_ref, cwg_ref, cwv_ref, cbg_ref, cbv_ref,
         wd_ref, wpg_ref, wpp_ref, gf_ref, y_ref, sg_ref, sv_ref, eg_ref, ev_ref, acc_ref) = refs
    t, c = pl.program_id(0), pl.program_id(1)
    tm = h_ref.shape[0]
    keep = CONV_W - 1
    n_split = FFN_ROW_SPLITS if prompt else 1
    rs = tm // n_split

    @pl.when(c == 0)
    def _():
        acc_ref[...] = jnp.zeros(acc_ref.shape, F32)

    if prompt:
        halo = hh_ref.shape[0]

        @pl.when(c == 0)
        def _():
            hext_ref[0:halo] = jnp.where(t > 0, hh_ref[...], jnp.zeros(hh_ref.shape, hh_ref.dtype))
            hext_ref[halo:] = h_ref[...]

        for q in range(n_split):
            rows = slice(0 if q == 0 else halo + q * rs, halo + (q + 1) * rs)
            hx = hext_ref[rows]
            eg_ref[rows] = jnp.dot(hx, wg_ref[...], preferred_element_type=F32)
            ev_ref[rows] = jnp.dot(hx, wv_ref[...], preferred_element_type=F32)
        sg_ref[...] = eg_ref[tm + halo - 8:, :]
        sv_ref[...] = ev_ref[tm + halo - 8:, :]

        def delayed(ref, i, q):
            return ref[pl.ds(halo + q * rs - i, rs), :]
    else:
        tb, lq = eg_ref.shape[0], eg_ref.shape[1] - 8
        fc = eg_ref.shape[2]
        hx = h_ref[...]
        eg_ref[:, 8:, :] = jnp.dot(hx, wg_ref[...], preferred_element_type=F32).reshape(tb, lq, fc)
        ev_ref[:, 8:, :] = jnp.dot(hx, wv_ref[...], preferred_element_type=F32).reshape(tb, lq, fc)
        eg_ref[:, 8 - keep:8, :] = hg_ref[...]
        ev_ref[:, 8 - keep:8, :] = hv_ref[...]
        sg_ref[...] = eg_ref[:, 8 + lq - keep:, :]
        sv_ref[...] = ev_ref[:, 8 + lq - keep:, :]

        def delayed(ref, i, q):
            return ref[:, pl.ds(8 - i, lq), :].reshape(tm, fc)

    def conv(ref, cw_ref, cb_ref, q):
        cw = cw_ref[...]
        out = cb_ref[...]
        for i in range(CONV_W):
            out = out + delayed(ref, CONV_W - 1 - i, q) * cw[i:i + 1, :]
        return out

    for q in range(n_split):
        gate = conv(eg_ref, cwg_ref, cbg_ref, q)
        val = conv(ev_ref, cwv_ref, cbv_ref, q)
        act = (gate * jax.nn.sigmoid(gate) * val).astype(BF16)
        acc_ref[q * rs:(q + 1) * rs] += jnp.dot(act, wd_ref[...], preferred_element_type=F32)

    @pl.when(c == pl.num_programs(1) - 1)
    def _():
        x3 = x2_ref[...] + acc_ref[...]
        gate_p = jax.nn.sigmoid(jnp.dot(x3.astype(BF16), wpg_ref[...], preferred_element_type=F32))
        emb = jnp.dot(p_ref[...].astype(BF16), wpp_ref[...], preferred_element_type=F32)
        y_ref[...] = _rmsnorm(x3 + gate_p * emb, gf_ref[...])


def _ffn_tail(h, x2, p, state, wup_bf, conv_w, conv_b, wdown_bf, wpg_bf, wpp_bf, g_final, prompt):
    rows, d = x2.shape
    d_ff = wdown_bf.shape[0]
    fc = FFN_CHUNK
    n_fc = d_ff // fc
    p_dim = p.shape[1]
    tm = min(FFN_ROWS, rows)
    n_t = rows // tm
    row_blk = lambda w: pl.BlockSpec((tm, w), lambda t, c: (t, 0))
    full = lambda arr: pl.BlockSpec(arr.shape, lambda t, c: (0,) * arr.ndim)
    gate_cols = lambda r: pl.BlockSpec((r, fc), lambda t, c: (0, c))
    val_cols = lambda r: pl.BlockSpec((r, fc), lambda t, c: (0, c + n_fc))
    weights = [gate_cols(d), val_cols(d), gate_cols(CONV_W), val_cols(CONV_W), gate_cols(1), val_cols(1),
               pl.BlockSpec((fc, d), lambda t, c: (c, 0)), full(wpg_bf), full(wpp_bf), full(g_final)]
    weight_args = (wup_bf, wup_bf, conv_w, conv_w, conv_b, conv_b, wdown_bf, wpg_bf, wpp_bf, g_final)
    y_shape = jax.ShapeDtypeStruct((rows, d), F32)
    if prompt:
        halo = BF16_SUBLANES
        acts = [row_blk(d),
                pl.BlockSpec((halo, d), lambda t, c: (jnp.maximum(t * (tm // halo) - 1, 0), 0)),
                row_blk(d), row_blk(p_dim)]
        act_args = (h, h, x2, p)
        st_shape = jax.ShapeDtypeStruct((n_t * 8, d_ff), F32)
        st_spec = pl.BlockSpec((8, fc), lambda t, c: (t, c))
        scratch = [pltpu.VMEM((tm + halo, d), BF16), pltpu.VMEM((tm + halo, fc), F32),
                   pltpu.VMEM((tm + halo, fc), F32), pltpu.VMEM((tm, d), F32)]
    else:
        n_seq, keep = state.shape[0], state.shape[1]
        lq = rows // n_seq
        hist_g = pl.BlockSpec((n_seq, keep, fc), lambda t, c: (0, 0, c))
        hist_v = pl.BlockSpec((n_seq, keep, fc), lambda t, c: (0, 0, c + n_fc))
        acts = [row_blk(d), hist_g, hist_v, row_blk(d), row_blk(p_dim)]
        act_args = (h, state, state, x2, p)
        st_shape = jax.ShapeDtypeStruct((n_seq, keep, d_ff), F32)
        st_spec = pl.BlockSpec((n_seq, keep, fc), lambda t, c: (0, 0, c))
        scratch = [pltpu.VMEM((n_seq, 8 + lq, fc), F32), pltpu.VMEM((n_seq, 8 + lq, fc), F32),
                   pltpu.VMEM((tm, d), F32)]
    return pl.pallas_call(
        functools.partial(_ffn_tail_kernel, prompt=prompt),
        grid=(n_t, n_fc),
        in_specs=acts + weights,
        out_specs=[row_blk(d), st_spec, st_spec],
        out_shape=[y_shape, st_shape, st_shape],
        scratch_shapes=scratch,
        compiler_params=_params("parallel", "arbitrary"),
        name="ffn_tail_prompt" if prompt else "ffn_tail_sample",
    )(*act_args, *weight_args)


def kernel(x_prompt, x_sample, cache_k, cache_v, state_pool, state_conv, page_table, p_prompt, p_sample,
           g_mix, w_in, w_out, w_pool, pool_scale, g_ffn, w_up, conv_w, conv_b, w_down, w_pg, w_pp, g_final):
    b_p, s_len, d = x_prompt.shape
    n_seq, lq, _ = x_sample.shape
    n_phys, depth, page, n_heads, head_dim = cache_k.shape
    attn_w = n_heads * head_dim
    pool_w = d - attn_w
    past_len = page_table.shape[1] * page
    hist = state_pool.shape[2]
    keep = state_conv.shape[2]
    d_ff = w_down.shape[1]
    assert b_p == 1 and depth == 1
    assert past_len % MOBA_BLOCK == 0 and MOBA_BLOCK % page == 0 and s_len % MOBA_BLOCK == 0
    assert lq == 8 and hist == max(POOL_WINDOWS) - 1 and keep == CONV_W - 1
    assert d_ff % FFN_CHUNK == 0

    i = 0
    row = lambda v: v.reshape(1, -1)
    w_in_bf, w_out_bf, w_pool_bf = w_in[i].astype(BF16), w_out[i].astype(BF16), w_pool[i].astype(BF16)
    w_up_bf, w_down_bf = w_up[i].astype(BF16), w_down[i].astype(BF16)
    w_pg_bf, w_pp_bf = w_pg[i].astype(BF16), w_pp[i].astype(BF16)

    xp = x_prompt.reshape(s_len, d)
    qt, k_p, v_p, u_p, kb, vt, kbar = _in_proj(xp, row(g_mix[i]), w_in_bf, jnp.arange(s_len),
                                               attn_w, head_dim, True)
    a_p = _moba_prompt(qt, kb, vt, kbar.reshape(-1, attn_w), head_dim)
    x2_p, h_p = _mix_out(xp, a_p, u_p, None, w_pool_bf, row(pool_scale[i]), w_out_bf, row(g_ffn[i]), True, 0)
    y_p, cg_p, cv_p = _ffn_tail(h_p, x2_p, p_prompt[i].reshape(s_len, -1), None, w_up_bf, conv_w[i],
                                row(conv_b[i]), w_down_bf, w_pg_bf, w_pp_bf, row(g_final), True)

    rows_s = n_seq * lq
    xs = x_sample.reshape(rows_s, d)
    pos_s = past_len + jnp.arange(rows_s) % lq
    q_s, k_s, v_s, u_s = _in_proj(xs, row(g_mix[i]), w_in_bf, pos_s, attn_w, head_dim, False)
    seq3 = lambda v: v.reshape(n_seq, lq, -1)
    a_s = _moba_sample(seq3(q_s), seq3(k_s), seq3(v_s), _pages_t(cache_k), _pages_t(cache_v), page_table,
                       n_heads, head_dim)
    x2_s, h_s, pool_s = _mix_out(xs, a_s.reshape(rows_s, attn_w), seq3(u_s), state_pool[:, i], w_pool_bf,
                                 row(pool_scale[i]), w_out_bf, row(g_ffn[i]), False, past_len)
    y_s, cg_s, cv_s = _ffn_tail(h_s, x2_s, p_sample[i].reshape(rows_s, -1), state_conv[:, i], w_up_bf,
                                conv_w[i], row(conv_b[i]), w_down_bf, w_pg_bf, w_pp_bf, row(g_final), False)

    heads5 = lambda v, n, l: v.reshape(n, 1, l, n_heads, head_dim)
    conv_p = jnp.concatenate([cg_p[-keep:], cv_p[-keep:]], axis=1).reshape(1, 1, keep, 2 * d_ff)
    conv_s = jnp.concatenate([cg_s, cv_s], axis=2).reshape(n_seq, 1, keep, 2 * d_ff)
    return (y_p.reshape(1, s_len, d), y_s.reshape(n_seq, lq, d),
            heads5(k_p, 1, s_len), heads5(v_p, 1, s_len), heads5(k_s, n_seq, lq), heads5(v_s, n_seq, lq),
            u_p[-hist:].reshape(1, 1, hist, pool_w), pool_s.reshape(n_seq, 1, hist, pool_w),
            conv_p, conv_s)
```

```python
import functools
import math

import jax
import jax.numpy as jnp
from jax import lax
from jax.experimental import pallas as pl
from jax.experimental.pallas import tpu as pltpu

F32 = jnp.float32
BF16 = jnp.bfloat16

MOBA_BLOCK = 256
MOBA_TOPK = 3
ROPE_THETA = 10000.0
POOL_WINDOWS = (2, 4, 8, 16)
CONV_W = 3
EPS = 1e-6
MASKED = -1e30
LOG2_E = math.log2(math.e)

LANES = 128
BF16_SUBLANES = 16
VMEM_LIMIT = 56 * 1024 * 1024

IN_PROJ_ROWS = 512
MIX_ROWS = 512
FFN_ROWS = 1024
FFN_CHUNK = 256
FFN_ROW_SPLITS = 4
PROMPT_SCORE_BUFFERS = 4
PROMPT_UNROLL = 8
SAMPLE_PAGES_PER_STEP = 32

_NT = (((1,), (1,)), ((), ()))


def _params(*sem):
    return pltpu.CompilerParams(dimension_semantics=sem, vmem_limit_bytes=VMEM_LIMIT)


def _rmsnorm(x, g):
    return x * lax.rsqrt(jnp.mean(x * x, axis=-1, keepdims=True) + EPS) * g


def _top_k_mask(s, idx, k, axis, n):
    picked = jnp.zeros(s.shape, jnp.bool_)
    for _ in range(k):
        mx = jnp.max(s, axis=axis, keepdims=True)
        first = jnp.min(jnp.where(s == mx, idx, float(n)), axis=axis, keepdims=True)
        hit = idx == first
        picked = picked | hit
        s = jnp.where(hit, -jnp.inf, s)
    return picked


def _in_proj_kernel(x_ref, g_ref, w_ref, cos_ref, sin_ref, *outs, attn_w, head_dim, prompt):
    tm = x_ref.shape[0]
    xn = _rmsnorm(x_ref[...], g_ref[...])
    z = jnp.dot(xn.astype(BF16), w_ref[...], preferred_element_type=F32)
    reps = attn_w // LANES
    cos = jnp.tile(cos_ref[...], (1, reps))
    sin = jnp.tile(sin_ref[...], (1, reps))
    half = head_dim // 2
    lane = lax.broadcasted_iota(jnp.int32, (tm, attn_w), 1)
    lower = (lane % head_dim) < half

    def rope(t):
        rot = jnp.where(lower, pltpu.roll(t, attn_w - half, 1), pltpu.roll(t, half, 1))
        return t * cos + rot * sin

    q = rope(z[:, :attn_w]) * (head_dim ** -0.5 * LOG2_E)
    k = rope(z[:, attn_w:2 * attn_w])
    v = z[:, 2 * attn_w:3 * attn_w]
    u = z[:, 3 * attn_w:]
    if prompt:
        qt_ref, k_ref, v_ref, u_ref, kb_ref, vt_ref, kbar_ref = outs
        qt_ref[...] = q.T
        kb_ref[...] = k.astype(BF16)
        vt = v.T.astype(BF16)
        ones = jnp.ones((BF16_SUBLANES, tm), BF16)
        vt = jnp.concatenate([piece for h in range(attn_w // head_dim)
                              for piece in (vt[h * head_dim:(h + 1) * head_dim], ones)], axis=0)
        for i in range(tm // MOBA_BLOCK):
            rows = slice(i * MOBA_BLOCK, (i + 1) * MOBA_BLOCK)
            vt_ref[i] = vt[:, rows]
            kbar_ref[i] = jnp.sum(k[rows], axis=0, keepdims=True) * (1.0 / MOBA_BLOCK)
    else:
        q_ref, k_ref, v_ref, u_ref = outs
        q_ref[...] = q
    k_ref[...] = k
    v_ref[...] = v
    u_ref[...] = u


def _rope_tables(pos, head_dim):
    inv = ROPE_THETA ** (-jnp.arange(0, head_dim, 2, dtype=F32) / head_dim)
    ang = pos.astype(F32)[:, None] * inv[None, :]
    c, s = jnp.cos(ang), jnp.sin(ang)
    reps = LANES // head_dim
    return jnp.tile(jnp.concatenate([c, c], 1), (1, reps)), jnp.tile(jnp.concatenate([-s, s], 1), (1, reps))


def _in_proj(x, g, w_bf, pos, attn_w, head_dim, prompt):
    rows, d = x.shape
    tm = min(IN_PROJ_ROWS, rows)
    cos, sin = _rope_tables(pos, head_dim)
    row_blk = lambda w: pl.BlockSpec((tm, w), lambda i: (i, 0))
    full = lambda a: pl.BlockSpec(a.shape, lambda i: (0,) * a.ndim)
    nat = jax.ShapeDtypeStruct((rows, attn_w), F32)
    if prompt:
        nblk, bpt = rows // MOBA_BLOCK, tm // MOBA_BLOCK
        vt_rows = attn_w // head_dim * (head_dim + BF16_SUBLANES)
        out_shape = [jax.ShapeDtypeStruct((attn_w, rows), F32), nat, nat, nat,
                     jax.ShapeDtypeStruct((rows, attn_w), BF16),
                     jax.ShapeDtypeStruct((nblk, vt_rows, MOBA_BLOCK), BF16),
                     jax.ShapeDtypeStruct((nblk, 1, attn_w), F32)]
        out_specs = [pl.BlockSpec((attn_w, tm), lambda i: (0, i)), row_blk(attn_w), row_blk(attn_w),
                     row_blk(attn_w), row_blk(attn_w),
                     pl.BlockSpec((bpt, vt_rows, MOBA_BLOCK), lambda i: (i, 0, 0)),
                     pl.BlockSpec((bpt, 1, attn_w), lambda i: (i, 0, 0))]
    else:
        out_shape = [nat, nat, nat, nat]
        out_specs = [row_blk(attn_w)] * 4
    return pl.pallas_call(
        functools.partial(_in_proj_kernel, attn_w=attn_w, head_dim=head_dim, prompt=prompt),
        grid=(rows // tm,),
        in_specs=[row_blk(d), full(g), full(w_bf), row_blk(LANES), row_blk(LANES)],
        out_specs=out_specs, out_shape=out_shape,
        compiler_params=_params("parallel"),
        name="in_proj_prompt" if prompt else "in_proj_sample",
    )(x, g, w_bf, cos, sin)


def _moba_prompt_kernel(qt_ref, kb_ref, vt_ref, kbar_ref, o_ref,
                        qm_ref, bias_ref, m_ref, acc_ref, s_all_ref, c_all_ref, *, head_dim, heads):
    t = pl.program_id(1)
    v_rows = head_dim + BF16_SUBLANES
    nblk = kbar_ref.shape[0]
    tq = qt_ref.shape[1]
    qt = qt_ref[...]
    row_head = lax.broadcasted_iota(jnp.int32, qt.shape, 0) // head_dim
    blk_idx = lax.broadcasted_iota(jnp.int32, (nblk, tq), 0)
    past = blk_idx < t
    key_pos = lax.broadcasted_iota(jnp.int32, (MOBA_BLOCK, tq), 0)
    qry_pos = lax.broadcasted_iota(jnp.int32, (MOBA_BLOCK, tq), 1)
    causal = key_pos <= qry_pos
    kbar = kbar_ref[...]
    k_own = kb_ref[t]
    topk = min(MOBA_TOPK, nblk)

    def v_t(j, h):
        return vt_ref[j, h * v_rows:(h + 1) * v_rows, :]

    bufs = [(s_all_ref.at[i], c_all_ref.at[i]) for i in range(s_all_ref.shape[0])]

    def score_head(j, buf, h):
        s_ref, c_ref = buf
        s = jnp.dot(kb_ref[j], qm_ref[h], preferred_element_type=F32)
        s_ref[h] = s
        c_ref[h] = jnp.max(s, axis=0, keepdims=True)

    def absorb_head(j, buf, h):
        s_ref, c_ref = buf
        bias = bias_ref[h, pl.ds(j, 1), :]
        m_prev = m_ref[h]
        m_new = jnp.maximum(m_prev, c_ref[h] + bias)
        alpha = jnp.exp2(m_prev - m_new)
        p = jnp.exp2(s_ref[h] - (m_new - bias))
        acc_ref[h] = alpha * acc_ref[h] + jnp.dot(v_t(j, h), p.astype(BF16),
                                                  preferred_element_type=F32)
        m_ref[h] = m_new

    def scores(j, buf):
        for h in range(heads):
            score_head(j, buf, h)

    def absorb(j, buf):
        for h in range(heads):
            absorb_head(j, buf, h)

    for h in range(heads):
        qm_ref[h] = jnp.where(row_head == h, qt, 0.0).astype(BF16)
    scores(0, bufs[0])
    for h in range(heads):
        qm = jnp.where(row_head == h, qt, 0.0)
        sc = jnp.dot(kbar, qm, precision=lax.Precision.HIGHEST, preferred_element_type=F32)
        picked = _top_k_mask(jnp.where(past, sc, -jnp.inf), blk_idx.astype(F32), topk, 0, nblk)
        bias_ref[h] = jnp.where(picked & past, 0.0, MASKED)
        s = jnp.dot(k_own, qm_ref[h], preferred_element_type=F32)
        s = jnp.where(causal, s, MASKED)
        m0 = jnp.max(s, axis=0, keepdims=True)
        p = jnp.exp2(s - m0)
        m_ref[h] = m0
        acc_ref[h] = jnp.dot(v_t(t, h), p.astype(BF16), preferred_element_type=F32)

    scores(jnp.minimum(1, nblk - 1), bufs[1])
    ahead = len(bufs) // 2

    def run(j0, n):
        for r in range(n):
            j_ahead = jnp.minimum(j0 + r + ahead, nblk - 1)
            for h in range(heads):
                score_head(j_ahead, bufs[(r + ahead) % len(bufs)], h)
                absorb_head(j0 + r, bufs[r % len(bufs)], h)

    def group(i, carry):
        run(PROMPT_UNROLL * i, PROMPT_UNROLL)
        return carry

    lax.fori_loop(0, t // PROMPT_UNROLL, group, 0)
    unroll = len(bufs)
    if PROMPT_UNROLL > unroll:
        @pl.when(t % PROMPT_UNROLL >= unroll)
        def _():
            run(t // PROMPT_UNROLL * PROMPT_UNROLL, unroll)
    base, rest = t // unroll * unroll, t % unroll
    for r in range(unroll - 1):
        if r >= ahead:
            @pl.when(rest > r)
            def _():
                scores(base + r, bufs[r % len(bufs)])

        @pl.when(rest > r)
        def _():
            absorb(base + r, bufs[r % len(bufs)])

    o = jnp.concatenate([acc_ref[h, :head_dim, :] / acc_ref[h, head_dim:head_dim + 1, :]
                         for h in range(heads)], axis=0)
    o_ref[...] = o.T.astype(o_ref.dtype)


def _moba_prompt(qt, kb, vt, kbar, head_dim):
    attn_w, s_len = qt.shape
    nblk = s_len // MOBA_BLOCK
    gw = 2 * LANES
    heads = gw // head_dim
    kb3 = kb.reshape(nblk, MOBA_BLOCK, attn_w)
    v_rows = head_dim + BF16_SUBLANES
    return pl.pallas_call(
        functools.partial(_moba_prompt_kernel, head_dim=head_dim, heads=heads),
        grid=(attn_w // gw, nblk),
        in_specs=[pl.BlockSpec((gw, MOBA_BLOCK), lambda g, t: (g, t)),
                  pl.BlockSpec((nblk, MOBA_BLOCK, gw), lambda g, t: (0, 0, g)),
                  pl.BlockSpec((nblk, heads * v_rows, MOBA_BLOCK), lambda g, t: (0, g, 0)),
                  pl.BlockSpec((nblk, gw), lambda g, t: (0, g))],
        out_specs=pl.BlockSpec((MOBA_BLOCK, gw), lambda g, t: (t, g)),
        out_shape=jax.ShapeDtypeStruct((s_len, attn_w), BF16),
        scratch_shapes=[pltpu.VMEM((heads, gw, MOBA_BLOCK), BF16),
                        pltpu.VMEM((heads, nblk, MOBA_BLOCK), F32),
                        pltpu.VMEM((heads, 1, MOBA_BLOCK), F32),
                        pltpu.VMEM((heads, v_rows, MOBA_BLOCK), F32),
                        pltpu.VMEM((PROMPT_SCORE_BUFFERS, heads, MOBA_BLOCK, MOBA_BLOCK), F32),
                        pltpu.VMEM((PROMPT_SCORE_BUFFERS, heads, 1, MOBA_BLOCK), F32)],
        compiler_params=_params("parallel", "arbitrary"),
        name="moba_prompt",
    )(qt, kb3, vt, kbar)


def _moba_sample_kernel(pt_ref, q_ref, kn_ref, vn_ref, *rest, pps, n_heads, head_dim):
    del pt_ref
    k_refs, v_refs = rest[:pps], rest[pps:2 * pps]
    (o_ref, s_ref, p_ref, sc_ref, acc_ref, qbd_ref, l_ref, pown_ref, btile_ref,
     kbf_ref, vbf_ref) = rest[2 * pps:]
    b, c = pl.program_id(0), pl.program_id(1)
    n_seq, n_chunks = pl.num_programs(0) - 1, s_ref.shape[0]
    lq, width = q_ref.shape
    page = k_refs[0].shape[1]
    ncol = n_heads * lq
    bps = pps * page // MOBA_BLOCK
    nblk = n_chunks * bps

    def stage(refs, dst_ref):
        for i in range(pps):
            dst_ref[:, i * page:(i + 1) * page] = refs[i][...].astype(BF16)

    def cols(jj):
        return slice(jj * MOBA_BLOCK, (jj + 1) * MOBA_BLOCK)

    @pl.when(b > 0)
    def _apply_values():
        stage(v_refs, vbf_ref)
        acc = lax.dot_general(p_ref[c], vbf_ref[...], _NT, preferred_element_type=F32)

        @pl.when(c == 0)
        def _():
            acc_ref[...] = acc

        @pl.when(c > 0)
        def _():
            acc_ref[...] += acc

        @pl.when(c == n_chunks - 1)
        def _finish():
            o = acc_ref[...]
            vn, pown = vn_ref[...], pown_ref[...]
            for j in range(lq):
                o = o + pown[:, j:j + 1] * vn[j:j + 1, :]
            o = o / l_ref[...]
            lane_head = lax.broadcasted_iota(jnp.int32, (lq, width), 1) // head_dim
            out = jnp.zeros((lq, width), F32)
            for h in range(n_heads):
                out = jnp.where(lane_head == h, o[h * lq:(h + 1) * lq, :], out)
            o_ref[...] = out.astype(o_ref.dtype)

    @pl.when(b < n_seq)
    def _score_keys():
        @pl.when(c == 0)
        def _():
            q_rep = jnp.concatenate([q_ref[...]] * n_heads, axis=0)
            row_head = lax.broadcasted_iota(jnp.int32, (ncol, width), 0) // lq
            lane_head = lax.broadcasted_iota(jnp.int32, (ncol, width), 1) // head_dim
            qbd = jnp.where(row_head == lane_head, q_rep, 0.0)
            qbd_ref[...] = qbd.astype(BF16)

        qbd = qbd_ref[...]
        stage(k_refs, kbf_ref)
        s = jnp.dot(qbd, kbf_ref[...], preferred_element_type=F32)
        s_ref[c] = s
        for jj in range(bps):
            sc_ref[c, :, jj:jj + 1] = jnp.sum(s[:, cols(jj)], axis=1, keepdims=True)

        @pl.when(c == n_chunks - 1)
        def _select_and_normalise():
            sc = jnp.concatenate([sc_ref[i] for i in range(n_chunks)], axis=1)
            blk_idx = lax.broadcasted_iota(jnp.int32, (ncol, nblk), 1)
            picked = _top_k_mask(sc, blk_idx.astype(F32), min(MOBA_TOPK, nblk), 1, nblk)
            bias = jnp.where(picked, 0.0, MASKED)
            s_own = lax.dot_general(qbd, kn_ref[...].astype(BF16), _NT, preferred_element_type=F32)
            qry = lax.broadcasted_iota(jnp.int32, (ncol, lq), 0) % lq
            key = lax.broadcasted_iota(jnp.int32, (ncol, lq), 1)
            s_own = jnp.where(key <= qry, s_own, MASKED)

            def lane_fold(x, op):
                out = x[:, :LANES]
                for i in range(1, MOBA_BLOCK // LANES):
                    out = op(out, x[:, i * LANES:(i + 1) * LANES])
                return out

            m_acc = jnp.full((ncol, LANES), MASKED, F32)
            for j in range(nblk):
                bias_j = jnp.broadcast_to(bias[:, j:j + 1], (ncol, LANES))
                btile_ref[j] = bias_j
                m_acc = jnp.maximum(m_acc, lane_fold(s_ref[j // bps, :, cols(j % bps)], jnp.maximum) + bias_j)
            m = jnp.maximum(jnp.max(s_own, axis=1, keepdims=True), jnp.max(m_acc, axis=1, keepdims=True))
            p_own = jnp.exp2(s_own - m)
            pown_ref[...] = p_own
            m_tile = jnp.broadcast_to(m, (ncol, LANES))
            l_acc = jnp.zeros((ncol, LANES), F32)
            for j in range(nblk):
                shift = jnp.tile(btile_ref[j] - m_tile, (1, MOBA_BLOCK // LANES))
                p = jnp.exp2(s_ref[j // bps, :, cols(j % bps)] + shift)
                l_acc = l_acc + lane_fold(p, jnp.add)
                p_ref[j // bps, :, cols(j % bps)] = p.astype(BF16)
            l_ref[...] = jnp.sum(p_own, axis=1, keepdims=True) + jnp.sum(l_acc, axis=1, keepdims=True)


def _pages_t(cache):
    n_phys, _, page, n_heads, head_dim = cache.shape
    return jnp.transpose(cache[:, 0], (0, 2, 3, 1)).reshape(n_phys, n_heads * head_dim, page)


def _moba_sample(q, k_new, v_new, cache_kt, cache_vt, page_table, n_heads, head_dim):
    n_seq, lq, width = q.shape
    page = cache_kt.shape[2]
    n_pages = page_table.shape[1]
    pps = min(SAMPLE_PAGES_PER_STEP, n_pages)
    n_chunks = n_pages // pps
    bps = pps * page // MOBA_BLOCK
    ncol = n_heads * lq
    last = n_seq - 1

    def seq_spec(idx):
        return pl.BlockSpec((None, lq, width), lambda b, c, pt: (idx(b), 0, 0))

    def page_spec(idx, i):
        return pl.BlockSpec((None, width, page),
                            lambda b, c, pt: (pt[idx(b) * n_pages + c * pps + i], 0, 0))

    cur = lambda b: jnp.minimum(b, last)
    prev = lambda b: jnp.maximum(b - 1, 0)
    grid_spec = pltpu.PrefetchScalarGridSpec(
        num_scalar_prefetch=1,
        grid=(n_seq + 1, n_chunks),
        in_specs=[seq_spec(cur), seq_spec(cur), seq_spec(prev)]
        + [page_spec(cur, i) for i in range(pps)] + [page_spec(prev, i) for i in range(pps)],
        out_specs=seq_spec(prev),
        scratch_shapes=[pltpu.VMEM((n_chunks, ncol, pps * page), F32),
                        pltpu.VMEM((n_chunks, ncol, pps * page), BF16),
                        pltpu.VMEM((n_chunks, ncol, bps), F32),
                        pltpu.VMEM((ncol, width), F32),
                        pltpu.VMEM((ncol, width), BF16),
                        pltpu.VMEM((ncol, 1), F32),
                        pltpu.VMEM((ncol, lq), F32),
                        pltpu.VMEM((n_chunks * bps, ncol, LANES), F32),
                        pltpu.VMEM((width, pps * page), BF16),
                        pltpu.VMEM((width, pps * page), BF16)])
    return pl.pallas_call(
        functools.partial(_moba_sample_kernel, pps=pps, n_heads=n_heads, head_dim=head_dim),
        grid_spec=grid_spec,
        out_shape=jax.ShapeDtypeStruct((n_seq, lq, width), BF16),
        compiler_params=_params("arbitrary", "arbitrary"),
        name="moba_sample",
    )(page_table.reshape(-1), q, k_new, v_new, *([cache_kt] * pps), *([cache_vt] * pps))


def _mix_out_kernel(*refs, prompt, start_pos, hist):
    if prompt:
        x_ref, a_ref, u_ref, uh_ref, wpool_ref, ps_ref, wout_ref, g_ref, x2_ref, h_ref, ext_ref = refs
        tm = u_ref.shape[0]
        t = pl.program_id(0)
        ext_ref[0:hist + 1] = jnp.where(t > 0, uh_ref[...], 0.0)
        ext_ref[hist + 1:] = u_ref[...]
        pos = t * tm + lax.broadcasted_iota(jnp.int32, (tm, 1), 0)

        def shifted(i, lanes):
            return ext_ref[pl.ds(hist + 1 - i, tm), lanes]
    else:
        (x_ref, a_ref, u_ref, uh_ref, wpool_ref, ps_ref, wout_ref, g_ref,
         x2_ref, h_ref, state_ref, ext_ref) = refs
        tb, lq = u_ref.shape[0], u_ref.shape[1]
        tm = tb * lq
        ext_ref[:, 1:hist + 1, :] = uh_ref[...]
        ext_ref[:, hist + 1:, :] = u_ref[...]
        state_ref[...] = ext_ref[:, lq + 1:, :]
        pos = start_pos + lax.broadcasted_iota(jnp.int32, (tm, 1), 0) % lq

        def shifted(i, lanes):
            return ext_ref[:, pl.ds(hist + 1 - i, lq), lanes].reshape(tm, lanes.stop - lanes.start)

    gw = wpool_ref.shape[1]
    ys = []
    for g, w in enumerate(POOL_WINDOWS):
        lanes = slice(g * gw, (g + 1) * gw)
        cur = shifted(0, lanes)
        total = cur
        for i in range(1, w):
            total = total + shifted(i, lanes)
        cnt = jnp.minimum(w, pos + 1).astype(F32)
        d = total / cnt - cur
        ys.append(jnp.dot(d.astype(BF16), wpool_ref[g], preferred_element_type=F32))
    pooled = jnp.concatenate(ys, axis=1) * ps_ref[...]
    mixed = jnp.concatenate([a_ref[...], pooled.astype(BF16)], axis=1)
    x2 = x_ref[...] + jnp.dot(mixed, wout_ref[...], preferred_element_type=F32)
    x2_ref[...] = x2
    h_ref[...] = _rmsnorm(x2, g_ref[...]).astype(BF16)


def _mix_out(x, a, u, u_hist, wpool_bf, pool_scale, wout_bf, g_ffn, prompt, start_pos):
    rows, d = x.shape
    pool_w = a.shape[1]
    hist = max(POOL_WINDOWS) - 1
    full = lambda arr: pl.BlockSpec(arr.shape, lambda i: (0,) * arr.ndim)
    row_blk = lambda w: pl.BlockSpec((tm, w), lambda i: (i, 0))
    out_shape = [jax.ShapeDtypeStruct((rows, d), F32), jax.ShapeDtypeStruct((rows, d), BF16)]
    if prompt:
        tm = min(MIX_ROWS, rows)
        halo = hist + 1
        u_specs = [row_blk(pool_w),
                   pl.BlockSpec((halo, pool_w), lambda i: (jnp.maximum(i * (tm // halo) - 1, 0), 0))]
        u_args = (u, u)
        out_specs = [row_blk(d), row_blk(d)]
        scratch = [pltpu.VMEM((tm + halo, pool_w), F32)]
        grid = (rows // tm,)
    else:
        n_seq, lq = u.shape[0], u.shape[1]
        tm = rows
        u_specs = [pl.BlockSpec((n_seq, lq, pool_w), lambda i: (0, 0, 0)),
                   pl.BlockSpec((n_seq, hist, pool_w), lambda i: (0, 0, 0))]
        u_args = (u, u_hist)
        out_shape.append(jax.ShapeDtypeStruct((n_seq, hist, pool_w), F32))
        out_specs = [row_blk(d), row_blk(d), pl.BlockSpec((n_seq, hist, pool_w), lambda i: (0, 0, 0))]
        scratch = [pltpu.VMEM((n_seq, hist + 1 + lq, pool_w), F32)]
        grid = (1,)
    return pl.pallas_call(
        functools.partial(_mix_out_kernel, prompt=prompt, start_pos=start_pos, hist=hist),
        grid=grid,
        in_specs=[row_blk(d), row_blk(a.shape[1])] + u_specs
        + [full(wpool_bf), full(pool_scale), full(wout_bf), full(g_ffn)],
        out_specs=out_specs, out_shape=out_shape, scratch_shapes=scratch,
        compiler_params=_params("parallel"),
        name="mix_out_prompt" if prompt else "mix_out_sample",
    )(x, a, *u_args, wpool_bf, pool_scale, wout_bf, g_ffn)


def _ffn_tail_kernel(*refs, prompt):
    if prompt:
        (h_ref, hh_ref, x2_ref, p_ref, wg_ref, wv_ref, cwg_ref, cwv_ref, cbg_ref, cbv_ref, wd_ref,
         wpg_ref, wpp_ref, gf_ref, y_ref, sg_ref, sv_ref, hext_ref, eg_ref, ev_ref, acc_ref) = refs
    else:
        (h_ref, hg_ref, hv_ref, x2_ref, p_ref, wg_ref, wv_ref, cwg_ref, cwv_ref, cbg_ref, cbv_ref,
         wd_ref, wpg_ref, wpp_ref, gf_ref, y_ref, sg_ref, sv_ref, eg_ref, ev_ref, acc_ref) = refs
    t, c = pl.program_id(0), pl.program_id(1)
    tm = h_ref.shape[0]
    keep = CONV_W - 1
    n_split = FFN_ROW_SPLITS if prompt else 1
    rs = tm // n_split

    @pl.when(c == 0)
    def _():
        acc_ref[...] = jnp.zeros(acc_ref.shape, F32)

    if prompt:
        halo = hh_ref.shape[0]

        @pl.when(c == 0)
        def _():
            hext_ref[0:halo] = jnp.where(t > 0, hh_ref[...], jnp.zeros(hh_ref.shape, hh_ref.dtype))
            hext_ref[halo:] = h_ref[...]

        for q in range(n_split):
            rows = slice(0 if q == 0 else halo + q * rs, halo + (q + 1) * rs)
            hx = hext_ref[rows]
            eg_ref[rows] = jnp.dot(hx, wg_ref[...], preferred_element_type=F32)
            ev_ref[rows] = jnp.dot(hx, wv_ref[...], preferred_element_type=F32)
        sg_ref[...] = eg_ref[tm + halo - 8:, :]
        sv_ref[...] = ev_ref[tm + halo - 8:, :]

        def delayed(ref, i, q):
            return ref[pl.ds(halo + q * rs - i, rs), :]
    else:
        tb, lq = eg_ref.shape[0], eg_ref.shape[1] - 8
        fc = eg_ref.shape[2]
        hx = h_ref[...]
        eg_ref[:, 8:, :] = jnp.dot(hx, wg_ref[...], preferred_element_type=F32).reshape(tb, lq, fc)
        ev_ref[:, 8:, :] = jnp.dot(hx, wv_ref[...], preferred_element_type=F32).reshape(tb, lq, fc)
        eg_ref[:, 8 - keep:8, :] = hg_ref[...]
        ev_ref[:, 8 - keep:8, :] = hv_ref[...]
        sg_ref[...] = eg_ref[:, 8 + lq - keep:, :]
        sv_ref[...] = ev_ref[:, 8 + lq - keep:, :]

        def delayed(ref, i, q):
            return ref[:, pl.ds(8 - i, lq), :].reshape(tm, fc)

    def conv(ref, cw_ref, cb_ref, q):
        cw = cw_ref[...]
        out = cb_ref[...]
        for i in range(CONV_W):
            out = out + delayed(ref, CONV_W - 1 - i, q) * cw[i:i + 1, :]
        return out

    for q in range(n_split):
        gate = conv(eg_ref, cwg_ref, cbg_ref, q)
        val = conv(ev_ref, cwv_ref, cbv_ref, q)
        act = (gate * jax.nn.sigmoid(gate) * val).astype(BF16)
        acc_ref[q * rs:(q + 1) * rs] += jnp.dot(act, wd_ref[...], preferred_element_type=F32)

    @pl.when(c == pl.num_programs(1) - 1)
    def _():
        x3 = x2_ref[...] + acc_ref[...]
        gate_p = jax.nn.sigmoid(jnp.dot(x3.astype(BF16), wpg_ref[...], preferred_element_type=F32))
        emb = jnp.dot(p_ref[...].astype(BF16), wpp_ref[...], preferred_element_type=F32)
        y_ref[...] = _rmsnorm(x3 + gate_p * emb, gf_ref[...])


def _ffn_tail(h, x2, p, state, wup_bf, conv_w, conv_b, wdown_bf, wpg_bf, wpp_bf, g_final, prompt):
    rows, d = x2.shape
    d_ff = wdown_bf.shape[0]
    fc = FFN_CHUNK
    n_fc = d_ff // fc
    p_dim = p.shape[1]
    tm = min(FFN_ROWS, rows)
    n_t = rows // tm
    row_blk = lambda w: pl.BlockSpec((tm, w), lambda t, c: (t, 0))
    full = lambda arr: pl.BlockSpec(arr.shape, lambda t, c: (0,) * arr.ndim)
    gate_cols = lambda r: pl.BlockSpec((r, fc), lambda t, c: (0, c))
    val_cols = lambda r: pl.BlockSpec((r, fc), lambda t, c: (0, c + n_fc))
    weights = [gate_cols(d), val_cols(d), gate_cols(CONV_W), val_cols(CONV_W), gate_cols(1), val_cols(1),
               pl.BlockSpec((fc, d), lambda t, c: (c, 0)), full(wpg_bf), full(wpp_bf), full(g_final)]
    weight_args = (wup_bf, wup_bf, conv_w, conv_w, conv_b, conv_b, wdown_bf, wpg_bf, wpp_bf, g_final)
    y_shape = jax.ShapeDtypeStruct((rows, d), F32)
    if prompt:
        halo = BF16_SUBLANES
        acts = [row_blk(d),
                pl.BlockSpec((halo, d), lambda t, c: (jnp.maximum(t * (tm // halo) - 1, 0), 0)),
                row_blk(d), row_blk(p_dim)]
        act_args = (h, h, x2, p)
        st_shape = jax.ShapeDtypeStruct((n_t * 8, d_ff), F32)
        st_spec = pl.BlockSpec((8, fc), lambda t, c: (t, c))
        scratch = [pltpu.VMEM((tm + halo, d), BF16), pltpu.VMEM((tm + halo, fc), F32),
                   pltpu.VMEM((tm + halo, fc), F32), pltpu.VMEM((tm, d), F32)]
    else:
        n_seq, keep = state.shape[0], state.shape[1]
        lq = rows // n_seq
        hist_g = pl.BlockSpec((n_seq, keep, fc), lambda t, c: (0, 0, c))
        hist_v = pl.BlockSpec((n_seq, keep, fc), lambda t, c: (0, 0, c + n_fc))
        acts = [row_blk(d), hist_g, hist_v, row_blk(d), row_blk(p_dim)]
        act_args = (h, state, state, x2, p)
        st_shape = jax.ShapeDtypeStruct((n_seq, keep, d_ff), F32)
        st_spec = pl.BlockSpec((n_seq, keep, fc), lambda t, c: (0, 0, c))
        scratch = [pltpu.VMEM((n_seq, 8 + lq, fc), F32), pltpu.VMEM((n_seq, 8 + lq, fc), F32),
                   pltpu.VMEM((tm, d), F32)]
    return pl.pallas_call(
        functools.partial(_ffn_tail_kernel, prompt=prompt),
        grid=(n_t, n_fc),
        in_specs=acts + weights,
        out_specs=[row_blk(d), st_spec, st_spec],
        out_shape=[y_shape, st_shape, st_shape],
        scratch_shapes=scratch,
        compiler_params=_params("parallel", "arbitrary"),
        name="ffn_tail_prompt" if prompt else "ffn_tail_sample",
    )(*act_args, *weight_args)


def kernel(x_prompt, x_sample, cache_k, cache_v, state_pool, state_conv, page_table, p_prompt, p_sample,
           g_mix, w_in, w_out, w_pool, pool_scale, g_ffn, w_up, conv_w, conv_b, w_down, w_pg, w_pp, g_final):
    b_p, s_len, d = x_prompt.shape
    n_seq, lq, _ = x_sample.shape
    n_phys, depth, page, n_heads, head_dim = cache_k.shape
    attn_w = n_heads * head_dim
    pool_w = d - attn_w
    past_len = page_table.shape[1] * page
    hist = state_pool.shape[2]
    keep = state_conv.shape[2]
    d_ff = w_down.shape[1]
    assert b_p == 1 and depth == 1
    assert past_len % MOBA_BLOCK == 0 and MOBA_BLOCK % page == 0 and s_len % MOBA_BLOCK == 0
    assert lq == 8 and hist == max(POOL_WINDOWS) - 1 and keep == CONV_W - 1
    assert d_ff % FFN_CHUNK == 0

    i = 0
    row = lambda v: v.reshape(1, -1)
    w_in_bf, w_out_bf, w_pool_bf = w_in[i].astype(BF16), w_out[i].astype(BF16), w_pool[i].astype(BF16)
    w_up_bf, w_down_bf = w_up[i].astype(BF16), w_down[i].astype(BF16)
    w_pg_bf, w_pp_bf = w_pg[i].astype(BF16), w_pp[i].astype(BF16)

    xp = x_prompt.reshape(s_len, d)
    qt, k_p, v_p, u_p, kb, vt, kbar = _in_proj(xp, row(g_mix[i]), w_in_bf, jnp.arange(s_len),
                                               attn_w, head_dim, True)
    a_p = _moba_prompt(qt, kb, vt, kbar.reshape(-1, attn_w), head_dim)
    x2_p, h_p = _mix_out(xp, a_p, u_p, None, w_pool_bf, row(pool_scale[i]), w_out_bf, row(g_ffn[i]), True, 0)
    y_p, cg_p, cv_p = _ffn_tail(h_p, x2_p, p_prompt[i].reshape(s_len, -1), None, w_up_bf, conv_w[i],
                                row(conv_b[i]), w_down_bf, w_pg_bf, w_pp_bf, row(g_final), True)

    rows_s = n_seq * lq
    xs = x_sample.reshape(rows_s, d)
    pos_s = past_len + jnp.arange(rows_s) % lq
    q_s, k_s, v_s, u_s = _in_proj(xs, row(g_mix[i]), w_in_bf, pos_s, attn_w, head_dim, False)
    seq3 = lambda v: v.reshape(n_seq, lq, -1)
    a_s = _moba_sample(seq3(q_s), seq3(k_s), seq3(v_s), _pages_t(cache_k), _pages_t(cache_v), page_table,
                       n_heads, head_dim)
    x2_s, h_s, pool_s = _mix_out(xs, a_s.reshape(rows_s, attn_w), seq3(u_s), state_pool[:, i], w_pool_bf,
                                 row(pool_scale[i]), w_out_bf, row(g_ffn[i]), False, past_len)
    y_s, cg_s, cv_s = _ffn_tail(h_s, x2_s, p_sample[i].reshape(rows_s, -1), state_conv[:, i], w_up_bf,
                                conv_w[i], row(conv_b[i]), w_down_bf, w_pg_bf, w_pp_bf, row(g_final), False)

    heads5 = lambda v, n, l: v.reshape(n, 1, l, n_heads, head_dim)
    conv_p = jnp.concatenate([cg_p[-keep:], cv_p[-keep:]], axis=1).reshape(1, 1, keep, 2 * d_ff)
    conv_s = jnp.concatenate([cg_s, cv_s], axis=2).reshape(n_seq, 1, keep, 2 * d_ff)
    return (y_p.reshape(1, s_len, d), y_s.reshape(n_seq, lq, d),
            heads5(k_p, 1, s_len), heads5(v_p, 1, s_len), heads5(k_s, n_seq, lq), heads5(v_s, n_seq, lq),
            u_p[-hist:].reshape(1, 1, hist, pool_w), pool_s.reshape(n_seq, 1, hist, pool_w),
            conv_p, conv_s)
```

```python
import functools
import math

import jax
import jax.numpy as jnp
from jax import lax
from jax.experimental import pallas as pl
from jax.experimental.pallas import tpu as pltpu

F32 = jnp.float32
BF16 = jnp.bfloat16

MOBA_BLOCK = 256
MOBA_TOPK = 3
ROPE_THETA = 10000.0
POOL_WINDOWS = (2, 4, 8, 16)
CONV_W = 3
EPS = 1e-6
MASKED = -1e30
LOG2_E = math.log2(math.e)

LANES = 128
BF16_SUBLANES = 16
VMEM_LIMIT = 56 * 1024 * 1024

IN_PROJ_ROWS = 512
MIX_ROWS = 512
FFN_ROWS = 1024
FFN_CHUNK = 256
FFN_ROW_SPLITS = 4
PROMPT_SCORE_BUFFERS = 4
PROMPT_UNROLL = 8
SAMPLE_PAGES_PER_STEP = 32

_NT = (((1,), (1,)), ((), ()))


def _params(*sem):
    return pltpu.CompilerParams(dimension_semantics=sem, vmem_limit_bytes=VMEM_LIMIT)


def _rmsnorm(x, g):
    return x * lax.rsqrt(jnp.mean(x * x, axis=-1, keepdims=True) + EPS) * g


def _top_k_mask(s, idx, k, axis, n):
    picked = jnp.zeros(s.shape, jnp.bool_)
    for _ in range(k):
        mx = jnp.max(s, axis=axis, keepdims=True)
        first = jnp.min(jnp.where(s == mx, idx, float(n)), axis=axis, keepdims=True)
        hit = idx == first
        picked = picked | hit
        s = jnp.where(hit, -jnp.inf, s)
    return picked


def _in_proj_kernel(x_ref, g_ref, w_ref, cos_ref, sin_ref, *outs, attn_w, head_dim, prompt):
    tm = x_ref.shape[0]
    xn = _rmsnorm(x_ref[...], g_ref[...])
    z = jnp.dot(xn.astype(BF16), w_ref[...], preferred_element_type=F32)
    reps = attn_w // LANES
    cos = jnp.tile(cos_ref[...], (1, reps))
    sin = jnp.tile(sin_ref[...], (1, reps))
    half = head_dim // 2
    lane = lax.broadcasted_iota(jnp.int32, (tm, attn_w), 1)
    lower = (lane % head_dim) < half

    def rope(t):
        rot = jnp.where(lower, pltpu.roll(t, attn_w - half, 1), pltpu.roll(t, half, 1))
        return t * cos + rot * sin

    q = rope(z[:, :attn_w]) * (head_dim ** -0.5 * LOG2_E)
    k = rope(z[:, attn_w:2 * attn_w])
    v = z[:, 2 * attn_w:3 * attn_w]
    u = z[:, 3 * attn_w:]
    if prompt:
        qt_ref, k_ref, v_ref, u_ref, kb_ref, vt_ref, kbar_ref = outs
        qt_ref[...] = q.T
        kb_ref[...] = k.astype(BF16)
        vt = v.T.astype(BF16)
        ones = jnp.ones((BF16_SUBLANES, tm), BF16)
        vt = jnp.concatenate([piece for h in range(attn_w // head_dim)
                              for piece in (vt[h * head_dim:(h + 1) * head_dim], ones)], axis=0)
        for i in range(tm // MOBA_BLOCK):
            rows = slice(i * MOBA_BLOCK, (i + 1) * MOBA_BLOCK)
            vt_ref[i] = vt[:, rows]
            kbar_ref[i] = jnp.sum(k[rows], axis=0, keepdims=True) * (1.0 / MOBA_BLOCK)
    else:
        q_ref, k_ref, v_ref, u_ref = outs
        q_ref[...] = q
    k_ref[...] = k
    v_ref[...] = v
    u_ref[...] = u


def _rope_tables(pos, head_dim):
    inv = ROPE_THETA ** (-jnp.arange(0, head_dim, 2, dtype=F32) / head_dim)
    ang = pos.astype(F32)[:, None] * inv[None, :]
    c, s = jnp.cos(ang), jnp.sin(ang)
    reps = LANES // head_dim
    return jnp.tile(jnp.concatenate([c, c], 1), (1, reps)), jnp.tile(jnp.concatenate([-s, s], 1), (1, reps))


def _in_proj(x, g, w_bf, pos, attn_w, head_dim, prompt):
    rows, d = x.shape
    tm = min(IN_PROJ_ROWS, rows)
    cos, sin = _rope_tables(pos, head_dim)
    row_blk = lambda w: pl.BlockSpec((tm, w), lambda i: (i, 0))
    full = lambda a: pl.BlockSpec(a.shape, lambda i: (0,) * a.ndim)
    nat = jax.ShapeDtypeStruct((rows, attn_w), F32)
    if prompt:
        nblk, bpt = rows // MOBA_BLOCK, tm // MOBA_BLOCK
        vt_rows = attn_w // head_dim * (head_dim + BF16_SUBLANES)
        out_shape = [jax.ShapeDtypeStruct((attn_w, rows), F32), nat, nat, nat,
                     jax.ShapeDtypeStruct((rows, attn_w), BF16),
                     jax.ShapeDtypeStruct((nblk, vt_rows, MOBA_BLOCK), BF16),
                     jax.ShapeDtypeStruct((nblk, 1, attn_w), F32)]
        out_specs = [pl.BlockSpec((attn_w, tm), lambda i: (0, i)), row_blk(attn_w), row_blk(attn_w),
                     row_blk(attn_w), row_blk(attn_w),
                     pl.BlockSpec((bpt, vt_rows, MOBA_BLOCK), lambda i: (i, 0, 0)),
                     pl.BlockSpec((bpt, 1, attn_w), lambda i: (i, 0, 0))]
    else:
        out_shape = [nat, nat, nat, nat]
        out_specs = [row_blk(attn_w)] * 4
    return pl.pallas_call(
        functools.partial(_in_proj_kernel, attn_w=attn_w, head_dim=head_dim, prompt=prompt),
        grid=(rows // tm,),
        in_specs=[row_blk(d), full(g), full(w_bf), row_blk(LANES), row_blk(LANES)],
        out_specs=out_specs, out_shape=out_shape,
        compiler_params=_params("parallel"),
        name="in_proj_prompt" if prompt else "in_proj_sample",
    )(x, g, w_bf, cos, sin)


def _moba_prompt_kernel(qt_ref, kb_ref, vt_ref, kbar_ref, o_ref,
                        qm_ref, bias_ref, m_ref, acc_ref, s_all_ref, c_all_ref, *, head_dim, heads):
    t = pl.program_id(1)
    v_rows = head_dim + BF16_SUBLANES
    nblk = kbar_ref.shape[0]
    tq = qt_ref.shape[1]
    qt = qt_ref[...]
    row_head = lax.broadcasted_iota(jnp.int32, qt.shape, 0) // head_dim
    blk_idx = lax.broadcasted_iota(jnp.int32, (nblk, tq), 0)
    past = blk_idx < t
    key_pos = lax.broadcasted_iota(jnp.int32, (MOBA_BLOCK, tq), 0)
    qry_pos = lax.broadcasted_iota(jnp.int32, (MOBA_BLOCK, tq), 1)
    causal = key_pos <= qry_pos
    kbar = kbar_ref[...]
    k_own = kb_ref[t]
    topk = min(MOBA_TOPK, nblk)

    def v_t(j, h):
        return vt_ref[j, h * v_rows:(h + 1) * v_rows, :]

    bufs = [(s_all_ref.at[i], c_all_ref.at[i]) for i in range(s_all_ref.shape[0])]

    def score_head(j, buf, h):
        s_ref, c_ref = buf
        s = jnp.dot(kb_ref[j], qm_ref[h], preferred_element_type=F32)
        s_ref[h] = s
        c_ref[h] = jnp.max(s, axis=0, keepdims=True)

    def absorb_head(j, buf, h):
        s_ref, c_ref = buf
        bias = bias_ref[h, pl.ds(j, 1), :]
        m_prev = m_ref[h]
        m_new = jnp.maximum(m_prev, c_ref[h] + bias)
        alpha = jnp.exp2(m_prev - m_new)
        p = jnp.exp2(s_ref[h] - (m_new - bias))
        acc_ref[h] = alpha * acc_ref[h] + jnp.dot(v_t(j, h), p.astype(BF16),
                                                  preferred_element_type=F32)
        m_ref[h] = m_new

    def scores(j, buf):
        for h in range(heads):
            score_head(j, buf, h)

    for h in range(heads):
        qm_ref[h] = jnp.where(row_head == h, qt, 0.0).astype(BF16)
    scores(0, bufs[0])
    for h in range(heads):
        qm = jnp.where(row_head == h, qt, 0.0)
        sc = jnp.dot(kbar, qm, precision=lax.Precision.HIGHEST, preferred_element_type=F32)
        picked = _top_k_mask(jnp.where(past, sc, -jnp.inf), blk_idx.astype(F32), topk, 0, nblk)
        bias_ref[h] = jnp.where(picked & past, 0.0, MASKED)
        s = jnp.dot(k_own, qm_ref[h], preferred_element_type=F32)
        s = jnp.where(causal, s, MASKED)
        m0 = jnp.max(s, axis=0, keepdims=True)
        p = jnp.exp2(s - m0)
        m_ref[h] = m0
        acc_ref[h] = jnp.dot(v_t(t, h), p.astype(BF16), preferred_element_type=F32)

    scores(jnp.minimum(1, nblk - 1), bufs[1])
    ahead = len(bufs) // 2

    def run(j0, n):
        for r in range(n):
            j = jnp.minimum(j0 + r, nblk - 1)
            j_ahead = jnp.minimum(j0 + r + ahead, nblk - 1)
            for h in range(heads):
                score_head(j_ahead, bufs[(r + ahead) % len(bufs)], h)
                absorb_head(j, bufs[r % len(bufs)], h)

    def group(i, carry):
        run(PROMPT_UNROLL * i, PROMPT_UNROLL)
        return carry

    lax.fori_loop(0, t // PROMPT_UNROLL, group, 0)
    base = t // PROMPT_UNROLL * PROMPT_UNROLL
    for k in range(PROMPT_UNROLL // len(bufs)):
        @pl.when(t - base > k * len(bufs))
        def _():
            run(base + k * len(bufs), len(bufs))

    o = jnp.concatenate([acc_ref[h, :head_dim, :] / acc_ref[h, head_dim:head_dim + 1, :]
                         for h in range(heads)], axis=0)
    o_ref[...] = o.T.astype(o_ref.dtype)


def _moba_prompt(qt, kb, vt, kbar, head_dim):
    attn_w, s_len = qt.shape
    nblk = s_len // MOBA_BLOCK
    gw = 2 * LANES
    heads = gw // head_dim
    kb3 = kb.reshape(nblk, MOBA_BLOCK, attn_w)
    v_rows = head_dim + BF16_SUBLANES
    return pl.pallas_call(
        functools.partial(_moba_prompt_kernel, head_dim=head_dim, heads=heads),
        grid=(attn_w // gw, nblk),
        in_specs=[pl.BlockSpec((gw, MOBA_BLOCK), lambda g, t: (g, t)),
                  pl.BlockSpec((nblk, MOBA_BLOCK, gw), lambda g, t: (0, 0, g)),
                  pl.BlockSpec((nblk, heads * v_rows, MOBA_BLOCK), lambda g, t: (0, g, 0)),
                  pl.BlockSpec((nblk, gw), lambda g, t: (0, g))],
        out_specs=pl.BlockSpec((MOBA_BLOCK, gw), lambda g, t: (t, g)),
        out_shape=jax.ShapeDtypeStruct((s_len, attn_w), BF16),
        scratch_shapes=[pltpu.VMEM((heads, gw, MOBA_BLOCK), BF16),
                        pltpu.VMEM((heads, nblk, MOBA_BLOCK), F32),
                        pltpu.VMEM((heads, 1, MOBA_BLOCK), F32),
                        pltpu.VMEM((heads, v_rows, MOBA_BLOCK), F32),
                        pltpu.VMEM((PROMPT_SCORE_BUFFERS, heads, MOBA_BLOCK, MOBA_BLOCK), F32),
                        pltpu.VMEM((PROMPT_SCORE_BUFFERS, heads, 1, MOBA_BLOCK), F32)],
        compiler_params=_params("parallel", "arbitrary"),
        name="moba_prompt",
    )(qt, kb3, vt, kbar)


def _moba_sample_kernel(pt_ref, q_ref, kn_ref, vn_ref, *rest, pps, n_heads, head_dim):
    del pt_ref
    k_refs, v_refs = rest[:pps], rest[pps:2 * pps]
    o_ref, s_ref, p_ref, kbar_ref, acc_ref, qbd_ref, qbf_ref, l_ref, pown_ref, btile_ref = rest[2 * pps:]
    b, c = pl.program_id(0), pl.program_id(1)
    n_seq, n_chunks = pl.num_programs(0) - 1, s_ref.shape[0]
    lq, width = q_ref.shape
    page = k_refs[0].shape[1]
    ncol = n_heads * lq
    ppb = MOBA_BLOCK // page
    bps = pps // ppb
    nblk = n_chunks * bps

    def block_t(refs, jj):
        return jnp.concatenate([refs[jj * ppb + i][...] for i in range(ppb)], axis=1)

    def cols(jj):
        return slice(jj * MOBA_BLOCK, (jj + 1) * MOBA_BLOCK)

    @pl.when(b > 0)
    def _apply_values():
        acc = jnp.zeros((ncol, width), F32)
        for jj in range(bps):
            acc += lax.dot_general(p_ref[c, :, cols(jj)], block_t(v_refs, jj).astype(BF16), _NT,
                                   preferred_element_type=F32)

        @pl.when(c == 0)
        def _():
            acc_ref[...] = acc

        @pl.when(c > 0)
        def _():
            acc_ref[...] += acc

        @pl.when(c == n_chunks - 1)
        def _finish():
            o = acc_ref[...]
            vn, pown = vn_ref[...], pown_ref[...]
            for j in range(lq):
                o = o + pown[:, j:j + 1] * vn[j:j + 1, :]
            o = o / l_ref[...]
            lane_head = lax.broadcasted_iota(jnp.int32, (lq, width), 1) // head_dim
            out = jnp.zeros((lq, width), F32)
            for h in range(n_heads):
                out = jnp.where(lane_head == h, o[h * lq:(h + 1) * lq, :], out)
            o_ref[...] = out.astype(o_ref.dtype)

    @pl.when(b < n_seq)
    def _score_keys():
        @pl.when(c == 0)
        def _():
            q_rep = jnp.concatenate([q_ref[...]] * n_heads, axis=0)
            row_head = lax.broadcasted_iota(jnp.int32, (ncol, width), 0) // lq
            lane_head = lax.broadcasted_iota(jnp.int32, (ncol, width), 1) // head_dim
            qbd = jnp.where(row_head == lane_head, q_rep, 0.0)
            qbf_ref[...] = qbd
            qbd_ref[...] = qbd.astype(BF16)

        qbd = qbd_ref[...]
        for jj in range(bps):
            kt = block_t(k_refs, jj)
            kbar_ref[c, :, jj:jj + 1] = jnp.sum(kt, axis=1, keepdims=True) * (1.0 / MOBA_BLOCK)
            s_ref[c, :, cols(jj)] = jnp.dot(qbd, kt.astype(BF16), preferred_element_type=F32)

        @pl.when(c == n_chunks - 1)
        def _select_and_normalise():
            kbar = jnp.concatenate([kbar_ref[i] for i in range(n_chunks)], axis=1)
            sc = jnp.dot(qbf_ref[...], kbar, precision=lax.Precision.HIGHEST,
                         preferred_element_type=F32)
            blk_idx = lax.broadcasted_iota(jnp.int32, (ncol, nblk), 1)
            picked = _top_k_mask(sc, blk_idx.astype(F32), min(MOBA_TOPK, nblk), 1, nblk)
            bias = jnp.where(picked, 0.0, MASKED)
            s_own = lax.dot_general(qbd, kn_ref[...].astype(BF16), _NT, preferred_element_type=F32)
            qry = lax.broadcasted_iota(jnp.int32, (ncol, lq), 0) % lq
            key = lax.broadcasted_iota(jnp.int32, (ncol, lq), 1)
            s_own = jnp.where(key <= qry, s_own, MASKED)

            def lane_fold(x, op):
                out = x[:, :LANES]
                for i in range(1, MOBA_BLOCK // LANES):
                    out = op(out, x[:, i * LANES:(i + 1) * LANES])
                return out

            m_acc = jnp.full((ncol, LANES), MASKED, F32)
            for j in range(nblk):
                bias_j = jnp.broadcast_to(bias[:, j:j + 1], (ncol, LANES))
                btile_ref[j] = bias_j
                m_acc = jnp.maximum(m_acc, lane_fold(s_ref[j // bps, :, cols(j % bps)], jnp.maximum) + bias_j)
            m = jnp.maximum(jnp.max(s_own, axis=1, keepdims=True), jnp.max(m_acc, axis=1, keepdims=True))
            p_own = jnp.exp2(s_own - m)
            pown_ref[...] = p_own
            m_tile = jnp.broadcast_to(m, (ncol, LANES))
            l_acc = jnp.zeros((ncol, LANES), F32)
            for j in range(nblk):
                shift = jnp.tile(btile_ref[j] - m_tile, (1, MOBA_BLOCK // LANES))
                p = jnp.exp2(s_ref[j // bps, :, cols(j % bps)] + shift)
                l_acc = l_acc + lane_fold(p, jnp.add)
                p_ref[j // bps, :, cols(j % bps)] = p.astype(BF16)
            l_ref[...] = jnp.sum(p_own, axis=1, keepdims=True) + jnp.sum(l_acc, axis=1, keepdims=True)


def _pages_t(cache):
    n_phys, _, page, n_heads, head_dim = cache.shape
    return jnp.transpose(cache[:, 0], (0, 2, 3, 1)).reshape(n_phys, n_heads * head_dim, page)


def _moba_sample(q, k_new, v_new, cache_kt, cache_vt, page_table, n_heads, head_dim):
    n_seq, lq, width = q.shape
    page = cache_kt.shape[2]
    n_pages = page_table.shape[1]
    pps = min(SAMPLE_PAGES_PER_STEP, n_pages)
    n_chunks = n_pages // pps
    bps = pps * page // MOBA_BLOCK
    ncol = n_heads * lq
    last = n_seq - 1

    def seq_spec(idx):
        return pl.BlockSpec((None, lq, width), lambda b, c, pt: (idx(b), 0, 0))

    def page_spec(idx, i):
        return pl.BlockSpec((None, width, page),
                            lambda b, c, pt: (pt[idx(b) * n_pages + c * pps + i], 0, 0))

    cur = lambda b: jnp.minimum(b, last)
    prev = lambda b: jnp.maximum(b - 1, 0)
    grid_spec = pltpu.PrefetchScalarGridSpec(
        num_scalar_prefetch=1,
        grid=(n_seq + 1, n_chunks),
        in_specs=[seq_spec(cur), seq_spec(cur), seq_spec(prev)]
        + [page_spec(cur, i) for i in range(pps)] + [page_spec(prev, i) for i in range(pps)],
        out_specs=seq_spec(prev),
        scratch_shapes=[pltpu.VMEM((n_chunks, ncol, pps * page), F32),
                        pltpu.VMEM((n_chunks, ncol, pps * page), BF16),
                        pltpu.VMEM((n_chunks, width, bps), F32),
                        pltpu.VMEM((ncol, width), F32),
                        pltpu.VMEM((ncol, width), BF16),
                        pltpu.VMEM((ncol, width), F32),
                        pltpu.VMEM((ncol, 1), F32),
                        pltpu.VMEM((ncol, lq), F32),
                        pltpu.VMEM((n_chunks * bps, ncol, LANES), F32)])
    return pl.pallas_call(
        functools.partial(_moba_sample_kernel, pps=pps, n_heads=n_heads, head_dim=head_dim),
        grid_spec=grid_spec,
        out_shape=jax.ShapeDtypeStruct((n_seq, lq, width), BF16),
        compiler_params=_params("arbitrary", "arbitrary"),
        name="moba_sample",
    )(page_table.reshape(-1), q, k_new, v_new, *([cache_kt] * pps), *([cache_vt] * pps))


def _mix_out_kernel(*refs, prompt, start_pos, hist):
    if prompt:
        x_ref, a_ref, u_ref, uh_ref, wpool_ref, ps_ref, wout_ref, g_ref, x2_ref, h_ref, ext_ref = refs
        tm = u_ref.shape[0]
        t = pl.program_id(0)
        ext_ref[0:hist + 1] = jnp.where(t > 0, uh_ref[...], 0.0)
        ext_ref[hist + 1:] = u_ref[...]
        pos = t * tm + lax.broadcasted_iota(jnp.int32, (tm, 1), 0)

        def shifted(i, lanes):
            return ext_ref[pl.ds(hist + 1 - i, tm), lanes]
    else:
        (x_ref, a_ref, u_ref, uh_ref, wpool_ref, ps_ref, wout_ref, g_ref,
         x2_ref, h_ref, state_ref, ext_ref) = refs
        tb, lq = u_ref.shape[0], u_ref.shape[1]
        tm = tb * lq
        ext_ref[:, 1:hist + 1, :] = uh_ref[...]
        ext_ref[:, hist + 1:, :] = u_ref[...]
        state_ref[...] = ext_ref[:, lq + 1:, :]
        pos = start_pos + lax.broadcasted_iota(jnp.int32, (tm, 1), 0) % lq

        def shifted(i, lanes):
            return ext_ref[:, pl.ds(hist + 1 - i, lq), lanes].reshape(tm, lanes.stop - lanes.start)

    gw = wpool_ref.shape[1]
    ys = []
    for g, w in enumerate(POOL_WINDOWS):
        lanes = slice(g * gw, (g + 1) * gw)
        cur = shifted(0, lanes)
        total = cur
        for i in range(1, w):
            total = total + shifted(i, lanes)
        cnt = jnp.minimum(w, pos + 1).astype(F32)
        d = total / cnt - cur
        ys.append(jnp.dot(d.astype(BF16), wpool_ref[g], preferred_element_type=F32))
    pooled = jnp.concatenate(ys, axis=1) * ps_ref[...]
    mixed = jnp.concatenate([a_ref[...], pooled.astype(BF16)], axis=1)
    x2 = x_ref[...] + jnp.dot(mixed, wout_ref[...], preferred_element_type=F32)
    x2_ref[...] = x2
    h_ref[...] = _rmsnorm(x2, g_ref[...]).astype(BF16)


def _mix_out(x, a, u, u_hist, wpool_bf, pool_scale, wout_bf, g_ffn, prompt, start_pos):
    rows, d = x.shape
    pool_w = a.shape[1]
    hist = max(POOL_WINDOWS) - 1
    full = lambda arr: pl.BlockSpec(arr.shape, lambda i: (0,) * arr.ndim)
    row_blk = lambda w: pl.BlockSpec((tm, w), lambda i: (i, 0))
    out_shape = [jax.ShapeDtypeStruct((rows, d), F32), jax.ShapeDtypeStruct((rows, d), BF16)]
    if prompt:
        tm = min(MIX_ROWS, rows)
        halo = hist + 1
        u_specs = [row_blk(pool_w),
                   pl.BlockSpec((halo, pool_w), lambda i: (jnp.maximum(i * (tm // halo) - 1, 0), 0))]
        u_args = (u, u)
        out_specs = [row_blk(d), row_blk(d)]
        scratch = [pltpu.VMEM((tm + halo, pool_w), F32)]
        grid = (rows // tm,)
    else:
        n_seq, lq = u.shape[0], u.shape[1]
        tm = rows
        u_specs = [pl.BlockSpec((n_seq, lq, pool_w), lambda i: (0, 0, 0)),
                   pl.BlockSpec((n_seq, hist, pool_w), lambda i: (0, 0, 0))]
        u_args = (u, u_hist)
        out_shape.append(jax.ShapeDtypeStruct((n_seq, hist, pool_w), F32))
        out_specs = [row_blk(d), row_blk(d), pl.BlockSpec((n_seq, hist, pool_w), lambda i: (0, 0, 0))]
        scratch = [pltpu.VMEM((n_seq, hist + 1 + lq, pool_w), F32)]
        grid = (1,)
    return pl.pallas_call(
        functools.partial(_mix_out_kernel, prompt=prompt, start_pos=start_pos, hist=hist),
        grid=grid,
        in_specs=[row_blk(d), row_blk(a.shape[1])] + u_specs
        + [full(wpool_bf), full(pool_scale), full(wout_bf), full(g_ffn)],
        out_specs=out_specs, out_shape=out_shape, scratch_shapes=scratch,
        compiler_params=_params("parallel"),
        name="mix_out_prompt" if prompt else "mix_out_sample",
    )(x, a, *u_args, wpool_bf, pool_scale, wout_bf, g_ffn)


def _ffn_tail_kernel(*refs, prompt):
    if prompt:
        (h_ref, hh_ref, x2_ref, p_ref, wg_ref, wv_ref, cwg_ref, cwv_ref, cbg_ref, cbv_ref, wd_ref,
         wpg_ref, wpp_ref, gf_ref, y_ref, sg_ref, sv_ref, hext_ref, eg_ref, ev_ref, acc_ref) = refs
    else:
        (h_ref, hg_ref, hv_ref, x2_ref, p_ref, wg_ref, wv_ref, cwg_ref, cwv_ref, cbg_ref, cbv_ref,
         wd_ref, wpg_ref, wpp_ref, gf_ref, y_ref, sg_ref, sv_ref, eg_ref, ev_ref, acc_ref) = refs
    t, c = pl.program_id(0), pl.program_id(1)
    tm = h_ref.shape[0]
    keep = CONV_W - 1
    n_split = FFN_ROW_SPLITS if prompt else 1
    rs = tm // n_split

    @pl.when(c == 0)
    def _():
        acc_ref[...] = jnp.zeros(acc_ref.shape, F32)

    if prompt:
        halo = hh_ref.shape[0]

        @pl.when(c == 0)
        def _():
            hext_ref[0:halo] = jnp.where(t > 0, hh_ref[...], jnp.zeros(hh_ref.shape, hh_ref.dtype))
            hext_ref[halo:] = h_ref[...]

        for q in range(n_split):
            rows = slice(0 if q == 0 else halo + q * rs, halo + (q + 1) * rs)
            hx = hext_ref[rows]
            eg_ref[rows] = jnp.dot(hx, wg_ref[...], preferred_element_type=F32)
            ev_ref[rows] = jnp.dot(hx, wv_ref[...], preferred_element_type=F32)
        sg_ref[...] = eg_ref[tm + halo - 8:, :]
        sv_ref[...] = ev_ref[tm + halo - 8:, :]

        def delayed(ref, i, q):
            return ref[pl.ds(halo + q * rs - i, rs), :]
    else:
        tb, lq = eg_ref.shape[0], eg_ref.shape[1] - 8
        fc = eg_ref.shape[2]
        hx = h_ref[...]
        eg_ref[:, 8:, :] = jnp.dot(hx, wg_ref[...], preferred_element_type=F32).reshape(tb, lq, fc)
        ev_ref[:, 8:, :] = jnp.dot(hx, wv_ref[...], preferred_element_type=F32).reshape(tb, lq, fc)
        eg_ref[:, 8 - keep:8, :] = hg_ref[...]
        ev_ref[:, 8 - keep:8, :] = hv_ref[...]
        sg_ref[...] = eg_ref[:, 8 + lq - keep:, :]
        sv_ref[...] = ev_ref[:, 8 + lq - keep:, :]

        def delayed(ref, i, q):
            return ref[:, pl.ds(8 - i, lq), :].reshape(tm, fc)

    def conv(ref, cw_ref, cb_ref, q):
        cw = cw_ref[...]
        out = cb_ref[...]
        for i in range(CONV_W):
            out = out + delayed(ref, CONV_W - 1 - i, q) * cw[i:i + 1, :]
        return out

    for q in range(n_split):
        gate = conv(eg_ref, cwg_ref, cbg_ref, q)
        val = conv(ev_ref, cwv_ref, cbv_ref, q)
        act = (gate * jax.nn.sigmoid(gate) * val).astype(BF16)
        acc_ref[q * rs:(q + 1) * rs] += jnp.dot(act, wd_ref[...], preferred_element_type=F32)

    @pl.when(c == pl.num_programs(1) - 1)
    def _():
        x3 = x2_ref[...] + acc_ref[...]
        gate_p = jax.nn.sigmoid(jnp.dot(x3.astype(BF16), wpg_ref[...], preferred_element_type=F32))
        emb = jnp.dot(p_ref[...].astype(BF16), wpp_ref[...], preferred_element_type=F32)
        y_ref[...] = _rmsnorm(x3 + gate_p * emb, gf_ref[...])


def _ffn_tail(h, x2, p, state, wup_bf, conv_w, conv_b, wdown_bf, wpg_bf, wpp_bf, g_final, prompt):
    rows, d = x2.shape
    d_ff = wdown_bf.shape[0]
    fc = FFN_CHUNK
    n_fc = d_ff // fc
    p_dim = p.shape[1]
    tm = min(FFN_ROWS, rows)
    n_t = rows // tm
    row_blk = lambda w: pl.BlockSpec((tm, w), lambda t, c: (t, 0))
    full = lambda arr: pl.BlockSpec(arr.shape, lambda t, c: (0,) * arr.ndim)
    gate_cols = lambda r: pl.BlockSpec((r, fc), lambda t, c: (0, c))
    val_cols = lambda r: pl.BlockSpec((r, fc), lambda t, c: (0, c + n_fc))
    weights = [gate_cols(d), val_cols(d), gate_cols(CONV_W), val_cols(CONV_W), gate_cols(1), val_cols(1),
               pl.BlockSpec((fc, d), lambda t, c: (c, 0)), full(wpg_bf), full(wpp_bf), full(g_final)]
    weight_args = (wup_bf, wup_bf, conv_w, conv_w, conv_b, conv_b, wdown_bf, wpg_bf, wpp_bf, g_final)
    y_shape = jax.ShapeDtypeStruct((rows, d), F32)
    if prompt:
        halo = BF16_SUBLANES
        acts = [row_blk(d),
                pl.BlockSpec((halo, d), lambda t, c: (jnp.maximum(t * (tm // halo) - 1, 0), 0)),
                row_blk(d), row_blk(p_dim)]
        act_args = (h, h, x2, p)
        st_shape = jax.ShapeDtypeStruct((n_t * 8, d_ff), F32)
        st_spec = pl.BlockSpec((8, fc), lambda t, c: (t, c))
        scratch = [pltpu.VMEM((tm + halo, d), BF16), pltpu.VMEM((tm + halo, fc), F32),
                   pltpu.VMEM((tm + halo, fc), F32), pltpu.VMEM((tm, d), F32)]
    else:
        n_seq, keep = state.shape[0], state.shape[1]
        lq = rows // n_seq
        hist_g = pl.BlockSpec((n_seq, keep, fc), lambda t, c: (0, 0, c))
        hist_v = pl.BlockSpec((n_seq, keep, fc), lambda t, c: (0, 0, c + n_fc))
        acts = [row_blk(d), hist_g, hist_v, row_blk(d), row_blk(p_dim)]
        act_args = (h, state, state, x2, p)
        st_shape = jax.ShapeDtypeStruct((n_seq, keep, d_ff), F32)
        st_spec = pl.BlockSpec((n_seq, keep, fc), lambda t, c: (0, 0, c))
        scratch = [pltpu.VMEM((n_seq, 8 + lq, fc), F32), pltpu.VMEM((n_seq, 8 + lq, fc), F32),
                   pltpu.VMEM((tm, d), F32)]
    return pl.pallas_call(
        functools.partial(_ffn_tail_kernel, prompt=prompt),
        grid=(n_t, n_fc),
        in_specs=acts + weights,
        out_specs=[row_blk(d), st_spec, st_spec],
        out_shape=[y_shape, st_shape, st_shape],
        scratch_shapes=scratch,
        compiler_params=_params("parallel", "arbitrary"),
        name="ffn_tail_prompt" if prompt else "ffn_tail_sample",
    )(*act_args, *weight_args)


def kernel(x_prompt, x_sample, cache_k, cache_v, state_pool, state_conv, page_table, p_prompt, p_sample,
           g_mix, w_in, w_out, w_pool, pool_scale, g_ffn, w_up, conv_w, conv_b, w_down, w_pg, w_pp, g_final):
    b_p, s_len, d = x_prompt.shape
    n_seq, lq, _ = x_sample.shape
    n_phys, depth, page, n_heads, head_dim = cache_k.shape
    attn_w = n_heads * head_dim
    pool_w = d - attn_w
    past_len = page_table.shape[1] * page
    hist = state_pool.shape[2]
    keep = state_conv.shape[2]
    d_ff = w_down.shape[1]
    assert b_p == 1 and depth == 1
    assert past_len % MOBA_BLOCK == 0 and MOBA_BLOCK % page == 0 and s_len % MOBA_BLOCK == 0
    assert lq == 8 and hist == max(POOL_WINDOWS) - 1 and keep == CONV_W - 1
    assert d_ff % FFN_CHUNK == 0

    i = 0
    row = lambda v: v.reshape(1, -1)
    w_in_bf, w_out_bf, w_pool_bf = w_in[i].astype(BF16), w_out[i].astype(BF16), w_pool[i].astype(BF16)
    w_up_bf, w_down_bf = w_up[i].astype(BF16), w_down[i].astype(BF16)
    w_pg_bf, w_pp_bf = w_pg[i].astype(BF16), w_pp[i].astype(BF16)

    xp = x_prompt.reshape(s_len, d)
    qt, k_p, v_p, u_p, kb, vt, kbar = _in_proj(xp, row(g_mix[i]), w_in_bf, jnp.arange(s_len),
                                               attn_w, head_dim, True)
    a_p = _moba_prompt(qt, kb, vt, kbar.reshape(-1, attn_w), head_dim)
    x2_p, h_p = _mix_out(xp, a_p, u_p, None, w_pool_bf, row(pool_scale[i]), w_out_bf, row(g_ffn[i]), True, 0)
    y_p, cg_p, cv_p = _ffn_tail(h_p, x2_p, p_prompt[i].reshape(s_len, -1), None, w_up_bf, conv_w[i],
                                row(conv_b[i]), w_down_bf, w_pg_bf, w_pp_bf, row(g_final), True)

    rows_s = n_seq * lq
    xs = x_sample.reshape(rows_s, d)
    pos_s = past_len + jnp.arange(rows_s) % lq
    q_s, k_s, v_s, u_s = _in_proj(xs, row(g_mix[i]), w_in_bf, pos_s, attn_w, head_dim, False)
    seq3 = lambda v: v.reshape(n_seq, lq, -1)
    a_s = _moba_sample(seq3(q_s), seq3(k_s), seq3(v_s), _pages_t(cache_k), _pages_t(cache_v), page_table,
                       n_heads, head_dim)
    x2_s, h_s, pool_s = _mix_out(xs, a_s.reshape(rows_s, attn_w), seq3(u_s), state_pool[:, i], w_pool_bf,
                                 row(pool_scale[i]), w_out_bf, row(g_ffn[i]), False, past_len)
    y_s, cg_s, cv_s = _ffn_tail(h_s, x2_s, p_sample[i].reshape(rows_s, -1), state_conv[:, i], w_up_bf,
                                conv_w[i], row(conv_b[i]), w_down_bf, w_pg_bf, w_pp_bf, row(g_final), False)

    heads5 = lambda v, n, l: v.reshape(n, 1, l, n_heads, head_dim)
    conv_p = jnp.concatenate([cg_p[-keep:], cv_p[-keep:]], axis=1).reshape(1, 1, keep, 2 * d_ff)
    conv_s = jnp.concatenate([cg_s, cv_s], axis=2).reshape(n_seq, 1, keep, 2 * d_ff)
    return (y_p.reshape(1, s_len, d), y_s.reshape(n_seq, lq, d),
            heads5(k_p, 1, s_len), heads5(v_p, 1, s_len), heads5(k_s, n_seq, lq), heads5(v_s, n_seq, lq),
            u_p[-hist:].reshape(1, 1, hist, pool_w), pool_s.reshape(n_seq, 1, hist, pool_w),
            conv_p, conv_s)
```
